```python
import math
import jax, jax.numpy as jnp
from jax import lax
import numpy as np

D_MODEL = 2048
BATCH = 8
SEQ = 2048
DEPTH = 1

D_MIX = D_MODEL
CONV_WIDTH = 4
LRU_WIDTH = D_MIX // 2
LRU_HEADS = 16
LRU_HEAD_DIM = LRU_WIDTH // LRU_HEADS
LRU_C = 8.0
SSD_WIDTH = D_MIX - LRU_WIDTH
SSD_HEAD_DIM = 64
SSD_HEADS = SSD_WIDTH // SSD_HEAD_DIM
SSD_GROUPS = 2
SSD_STATE = 128
SSD_CHUNK = 128
SSD_CONV_DIM = SSD_WIDTH + 2 * SSD_GROUPS * SSD_STATE
IN_SPLITS = (LRU_WIDTH, LRU_WIDTH, SSD_WIDTH, SSD_CONV_DIM, SSD_HEADS)
IN_PROJ_DIM = sum(IN_SPLITS)
PEER_HEADS = 8
PEER_N_KEYS = 128
PEER_N_EXPERTS = PEER_N_KEYS * PEER_N_KEYS
PEER_QUERY_DIM = 256
PEER_HALF = PEER_QUERY_DIM // 2
PEER_TOPK = 16
PEER_TOKEN_BLOCK = 128
EPS = 1e-6

kernel_name = "hybrid_rglru_ssd_peer"


def rmsnorm(x, w):
    xf = x.astype(jnp.float32)
    y = xf * lax.rsqrt(jnp.mean(xf * xf, axis=-1, keepdims=True) + EPS)
    return (y * w.astype(jnp.float32)).astype(x.dtype)


def causal_depthwise_conv(x, w, b):
    c = x.shape[-1]
    y = lax.conv_general_dilated(
        x, w[:, None, :].astype(x.dtype), window_strides=(1,),
        padding=[(CONV_WIDTH - 1, 0)], dimension_numbers=('NWC', 'WIO', 'NWC'),
        feature_group_count=c)
    return y + b.astype(x.dtype)


def rglru(xb, wa, ba, wx, bx, lam):
    bsz, s, w = xb.shape
    xh = xb.reshape(bsz, s, LRU_HEADS, LRU_HEAD_DIM)
    r = jax.nn.sigmoid((jnp.einsum('bshi,hij->bshj', xh, wa).reshape(bsz, s, w) + ba).astype(jnp.float32))
    i = jax.nn.sigmoid((jnp.einsum('bshi,hij->bshj', xh, wx).reshape(bsz, s, w) + bx).astype(jnp.float32))
    log_a = -LRU_C * r * jax.nn.softplus(-lam.astype(jnp.float32))
    a = jnp.exp(log_a)
    b = jnp.sqrt(-jnp.expm1(2.0 * log_a)) * (i * xb.astype(jnp.float32))

    def step(h, ab):
        a_t, b_t = ab
        h = a_t * h + b_t
        return h, h

    _, hs = lax.scan(step, jnp.zeros((bsz, w), jnp.float32),
                     (a.transpose(1, 0, 2), b.transpose(1, 0, 2)))
    return hs.transpose(1, 0, 2).astype(xb.dtype)


def segsum(x):
    t = x.shape[-1]
    xx = jnp.broadcast_to(x[..., None], x.shape + (t,))
    xx = jnp.where(jnp.tril(jnp.ones((t, t), bool), -1), xx, 0.0)
    ss = jnp.cumsum(xx, axis=-2)
    return jnp.where(jnp.tril(jnp.ones((t, t), bool), 0), ss, -jnp.inf)


def ssd_chunked(x, dt, a, bm, cm):
    bsz, s, h, p = x.shape
    nc = s // SSD_CHUNK
    rep = h // SSD_GROUPS
    bh = jnp.repeat(bm, rep, axis=2).reshape(bsz, nc, SSD_CHUNK, h, SSD_STATE)
    ch = jnp.repeat(cm, rep, axis=2).reshape(bsz, nc, SSD_CHUNK, h, SSD_STATE)
    xc = (x * dt[..., None]).reshape(bsz, nc, SSD_CHUNK, h, p)
    adt = (a * dt).reshape(bsz, nc, SSD_CHUNK, h).transpose(0, 3, 1, 2)
    a_cs = jnp.cumsum(adt, axis=-1)
    lmat = jnp.exp(segsum(adt))
    scores = jnp.einsum('bclhn,bcshn->bhcls', ch, bh) * lmat
    y_diag = jnp.einsum('bhcls,bcshp->bclhp', scores, xc)
    decay_states = jnp.exp(a_cs[..., -1:] - a_cs)
    states = jnp.einsum('bclhn,bhcl,bclhp->bchpn', bh, decay_states, xc)
    states = jnp.concatenate([jnp.zeros_like(states[:, :1]), states], axis=1)
    chunk_decay = jnp.exp(segsum(jnp.pad(a_cs[..., -1], ((0, 0), (0, 0), (1, 0)))))
    states = jnp.einsum('bhzc,bchpn->bzhpn', chunk_decay, states)[:, :-1]
    y_off = jnp.einsum('bclhn,bchpn,bhcl->bclhp', ch, states, jnp.exp(a_cs))
    return (y_diag + y_off).reshape(bsz, s, h, p)


def peer(h, wq, sub_keys, u, v):
    bsz, s, d = h.shape
    q = (h @ wq).astype(jnp.float32).reshape(bsz, s, PEER_HEADS, 2, PEER_HALF)
    sc = jnp.einsum('bshkd,hknd->bshkn', q, sub_keys.astype(jnp.float32))
    s1, i1 = lax.top_k(sc[..., 0, :], PEER_TOPK)
    s2, i2 = lax.top_k(sc[..., 1, :], PEER_TOPK)
    cand_s = (s1[..., :, None] + s2[..., None, :]).reshape(bsz, s, PEER_HEADS, PEER_TOPK * PEER_TOPK)
    cand_i = (i1[..., :, None] * PEER_N_KEYS + i2[..., None, :]).reshape(bsz, s, PEER_HEADS, PEER_TOPK * PEER_TOPK)
    top_s, top_pos = lax.top_k(cand_s, PEER_TOPK)
    idx = jnp.take_along_axis(cand_i, top_pos, axis=-1)
    g = jax.nn.softmax(top_s, axis=-1)
    n_blocks = (bsz * s) // PEER_TOKEN_BLOCK
    hb = h.reshape(n_blocks, PEER_TOKEN_BLOCK, d)
    ib = idx.reshape(n_blocks, PEER_TOKEN_BLOCK, PEER_HEADS * PEER_TOPK)
    gb = g.astype(h.dtype).reshape(n_blocks, PEER_TOKEN_BLOCK, PEER_HEADS * PEER_TOPK)

    def block(args):
        hx, ix, gx = args
        u_sel = jnp.take(u, ix, axis=0)
        act = jax.nn.gelu(jnp.einsum('tkd,td->tk', u_sel, hx))
        v_sel = jnp.take(v, ix, axis=0)
        return jnp.einsum('tk,tkd->td', gx * act, v_sel)

    out = lax.map(block, (hb, ib, gb))
    return out.reshape(bsz, s, d)


def setup_inputs(seed: int = 0) -> dict:
    key = jax.random.key(seed)
    ks = jax.random.split(key, 32)
    f32 = jnp.float32

    def nrm(k, shape, scale):
        return jax.random.normal(k, shape, f32) * scale

    def gain(k, shape):
        return 1.0 + 0.02 * jax.random.normal(k, shape, f32)

    a8 = jax.random.uniform(ks[10], (DEPTH, LRU_WIDTH), f32, 0.9, 0.999)
    sig = a8 ** (1.0 / LRU_C)
    lru_lambda = jnp.log(sig) - jnp.log1p(-sig)
    dt0 = jnp.exp(jax.random.uniform(ks[13], (DEPTH, SSD_HEADS), f32, math.log(1e-3), math.log(1e-1)))
    ssd_dt_bias = dt0 + jnp.log(-jnp.expm1(-dt0))
    ssd_a_log = jnp.log(jax.random.uniform(ks[14], (DEPTH, SSD_HEADS), f32, 1.0, 16.0))
    return {
        'x': nrm(ks[0], (BATCH, SEQ, D_MODEL), 1.0),
        'norm_mix_w': gain(ks[1], (DEPTH, D_MODEL)),
        'w_in': nrm(ks[2], (DEPTH, D_MODEL, IN_PROJ_DIM), D_MODEL ** -0.5),
        'lru_conv_w': nrm(ks[3], (DEPTH, CONV_WIDTH, LRU_WIDTH), CONV_WIDTH ** -0.5),
        'lru_conv_b': nrm(ks[4], (DEPTH, LRU_WIDTH), 0.02),
        'lru_wa': nrm(ks[5], (DEPTH, LRU_HEADS, LRU_HEAD_DIM, LRU_HEAD_DIM), LRU_HEAD_DIM ** -0.5),
        'lru_ba': nrm(ks[6], (DEPTH, LRU_WIDTH), 0.02),
        'lru_wx': nrm(ks[7], (DEPTH, LRU_HEADS, LRU_HEAD_DIM, LRU_HEAD_DIM), LRU_HEAD_DIM ** -0.5),
        'lru_bx': nrm(ks[8], (DEPTH, LRU_WIDTH), 0.02),
        'lru_lambda': lru_lambda,
        'ssd_conv_w': nrm(ks[11], (DEPTH, CONV_WIDTH, SSD_CONV_DIM), CONV_WIDTH ** -0.5),
        'ssd_conv_b': nrm(ks[12], (DEPTH, SSD_CONV_DIM), 0.02),
        'ssd_dt_bias': ssd_dt_bias,
        'ssd_a_log': ssd_a_log,
        'ssd_d': gain(ks[15], (DEPTH, SSD_HEADS)),
        'ssd_norm_w': gain(ks[16], (DEPTH, SSD_WIDTH)),
        'w_out': nrm(ks[17], (DEPTH, D_MIX, D_MODEL), D_MIX ** -0.5),
        'norm_ffn_w': gain(ks[18], (DEPTH, D_MODEL)),
        'peer_wq': nrm(ks[19], (DEPTH, D_MODEL, PEER_HEADS * PEER_QUERY_DIM), D_MODEL ** -0.5),
        'peer_sub_keys': nrm(ks[20], (DEPTH, PEER_HEADS, 2, PEER_N_KEYS, PEER_HALF), PEER_HALF ** -0.5),
        'peer_u': nrm(ks[21], (DEPTH, PEER_N_EXPERTS, D_MODEL), D_MODEL ** -0.5),
        'peer_v': nrm(ks[22], (DEPTH, PEER_N_EXPERTS, D_MODEL), 0.1),
        'norm_final_w': gain(ks[23], (D_MODEL,)),
    }


def reference(x, norm_mix_w, w_in, lru_conv_w, lru_conv_b, lru_wa, lru_ba, lru_wx, lru_bx,
              lru_lambda, ssd_conv_w, ssd_conv_b, ssd_dt_bias, ssd_a_log, ssd_d, ssd_norm_w,
              w_out, norm_ffn_w, peer_wq, peer_sub_keys, peer_u, peer_v, norm_final_w):
    bsz, s, _ = x.shape
    split_at = [int(v) for v in np.cumsum(IN_SPLITS)[:-1]]
    for l in range(DEPTH):
        h = rmsnorm(x, norm_mix_w[l])
        proj = h @ w_in[l]
        lru_x, lru_gate, ssd_z, ssd_xbc, ssd_dt = jnp.split(proj, split_at, axis=-1)
        xl = causal_depthwise_conv(lru_x, lru_conv_w[l], lru_conv_b[l])
        y_lru = rglru(xl, lru_wa[l], lru_ba[l], lru_wx[l], lru_bx[l], lru_lambda[l]) * jax.nn.gelu(lru_gate)
        xbc = jax.nn.silu(causal_depthwise_conv(ssd_xbc, ssd_conv_w[l], ssd_conv_b[l]))
        xs, bm, cm = jnp.split(xbc, [SSD_WIDTH, SSD_WIDTH + SSD_GROUPS * SSD_STATE], axis=-1)
        xs_h = xs.astype(jnp.float32).reshape(bsz, s, SSD_HEADS, SSD_HEAD_DIM)
        dt = jax.nn.softplus(ssd_dt.astype(jnp.float32) + ssd_dt_bias[l].astype(jnp.float32))
        a = -jnp.exp(ssd_a_log[l].astype(jnp.float32))
        y = ssd_chunked(xs_h, dt, a,
                        bm.astype(jnp.float32).reshape(bsz, s, SSD_GROUPS, SSD_STATE),
                        cm.astype(jnp.float32).reshape(bsz, s, SSD_GROUPS, SSD_STATE))
        y = y + ssd_d[l].astype(jnp.float32)[:, None] * xs_h
        y = y.reshape(bsz, s, SSD_WIDTH).astype(x.dtype)
        y_ssd = rmsnorm(y * jax.nn.silu(ssd_z), ssd_norm_w[l])
        x = x + jnp.concatenate([y_lru, y_ssd], axis=-1) @ w_out[l]
        h = rmsnorm(x, norm_ffn_w[l])
        x = x + peer(h, peer_wq[l], peer_sub_keys[l], peer_u[l], peer_v[l])
    return rmsnorm(x, norm_final_w)
```

```python
import functools
import math

import jax
import jax.numpy as jnp
from jax import lax
from jax.experimental import pallas as pl
from jax.experimental.pallas import tpu as pltpu

F32 = jnp.float32
BF16 = jnp.bfloat16

D_MODEL = 2048
CONV_WIDTH = 4
LRU_WIDTH = 1024
LRU_HEADS = 16
LRU_HEAD_DIM = 64
LRU_C = 8.0
SSD_WIDTH = 1024
SSD_HEAD_DIM = 64
SSD_HEADS = 16
SSD_GROUPS = 2
SSD_STATE = 128
SSD_CONV_DIM = SSD_WIDTH + 2 * SSD_GROUPS * SSD_STATE
PEER_HEADS = 8
PEER_N_KEYS = 128
PEER_HALF = 128
PEER_TOPK = 16
EPS = 1e-6

LANES = 128
SUBLANES = 8
DT_PAD = LANES
PROJ_PAD = 3 * 1024 + SSD_CONV_DIM + DT_PAD
GATE_CHUNK = 256

IN_TILE = 256
SEQ_TILE = 256
SSD_CHUNK = 128
SCORE_TILE = 256
ROUTER_TILE = 128
PEER_TQ = 512
PEER_TE = 512
VMEM_LIMIT = 56 * 1024 * 1024


def _rms(x, w):
    return x * lax.rsqrt(jnp.mean(x * x, axis=-1, keepdims=True) + EPS) * w


def _in_proj_kernel(x_ref, nw_ref, w_ref, o_ref):
    hn = _rms(x_ref[...], nw_ref[...])
    o_ref[...] = jnp.dot(hn.astype(BF16), w_ref[...], preferred_element_type=F32)


def _in_proj(x2, norm_w, w_in_pad):
    t = x2.shape[0]
    return pl.pallas_call(
        _in_proj_kernel,
        grid=(t // IN_TILE,),
        in_specs=[
            pl.BlockSpec((IN_TILE, D_MODEL), lambda i: (i, 0)),
            pl.BlockSpec((1, D_MODEL), lambda i: (0, 0)),
            pl.BlockSpec((D_MODEL, PROJ_PAD), lambda i: (0, 0), pipeline_mode=pl.Buffered(1)),
        ],
        out_specs=pl.BlockSpec((IN_TILE, PROJ_PAD), lambda i: (i, 0)),
        out_shape=jax.ShapeDtypeStruct((t, PROJ_PAD), F32),
        compiler_params=pltpu.CompilerParams(
            dimension_semantics=("arbitrary",), vmem_limit_bytes=VMEM_LIMIT),
        name="in_proj",
    )(x2, norm_w, w_in_pad)


def _shift_rows(x, d, fill):
    rolled = pltpu.roll(x, d, axis=0)
    row = lax.broadcasted_iota(jnp.int32, x.shape, 0)
    return jnp.where(row < d, fill, rolled)


def _causal_conv(cur, prev8, w_ref, b_ref):
    row8 = lax.broadcasted_iota(jnp.int32, prev8.shape, 0)
    out = cur * w_ref[CONV_WIDTH - 1:CONV_WIDTH, :] + b_ref[...]
    for d in range(1, CONV_WIDTH):
        rolled = pltpu.roll(cur, d, axis=0)
        top = jnp.where(row8 < d, pltpu.roll(prev8, d, axis=0), rolled[:SUBLANES])
        shifted = jnp.concatenate([top, rolled[SUBLANES:]], axis=0)
        k = CONV_WIDTH - 1 - d
        out = out + shifted * w_ref[k:k + 1, :]
    return out


def _mixer_kernel(lx_ref, lg_ref, z_ref, xbc_ref, dt_ref,
                  lcw_ref, lcb_ref, wg_ref, ba_ref, bx_ref, lam_ref,
                  scw_ref, scb_ref, dtb_ref, alog_ref, dexp_ref, nw_ref, expand_ref,
                  y_ref,
                  lprev_ref, sprev_ref, h_ref, state_ref):
    ts = lx_ref.shape[0]

    @pl.when(pl.program_id(1) == 0)
    def _():
        lprev_ref[...] = jnp.zeros_like(lprev_ref)
        sprev_ref[...] = jnp.zeros_like(sprev_ref)
        h_ref[...] = jnp.zeros_like(h_ref)
        state_ref[...] = jnp.zeros_like(state_ref)

    lx = lx_ref[...]
    xl = _causal_conv(lx, lprev_ref[...], lcw_ref, lcb_ref)
    lprev_ref[...] = lx[ts - SUBLANES:, :]
    xl_bf = xl.astype(BF16)
    ra, ri = [], []
    for c in range(LRU_WIDTH // GATE_CHUNK):
        g = jnp.dot(xl_bf[:, c * GATE_CHUNK:(c + 1) * GATE_CHUNK], wg_ref[c],
                    preferred_element_type=F32)
        ra.append(g[:, :GATE_CHUNK])
        ri.append(g[:, GATE_CHUNK:])
    r = jax.nn.sigmoid(jnp.concatenate(ra, axis=1) + ba_ref[...])
    i = jax.nn.sigmoid(jnp.concatenate(ri, axis=1) + bx_ref[...])
    log_a = -LRU_C * r * jax.nn.softplus(-lam_ref[...])
    a = jnp.exp(log_a)
    b = jnp.sqrt(jnp.tanh(-log_a) * (1.0 + a * a)) * (i * xl)
    d = 1
    while d < ts:
        b = a * _shift_rows(b, d, 0.0) + b
        a = a * _shift_rows(a, d, 1.0)
        d *= 2
    hs = a * h_ref[0:1, :] + b
    h_ref[...] = jnp.broadcast_to(hs[ts - 1:ts, :], h_ref.shape)
    y_ref[:, :LRU_WIDTH] = (hs * jax.nn.gelu(lg_ref[...])).astype(y_ref.dtype)

    xbc_raw = xbc_ref[...]
    xbc = jax.nn.silu(_causal_conv(xbc_raw, sprev_ref[...], scw_ref, scb_ref))
    sprev_ref[...] = xbc_raw[ts - SUBLANES:, :]
    xs = xbc[:, :SSD_WIDTH]
    dt = jax.nn.softplus(dt_ref[...] + dtb_ref[...])
    adt = -jnp.exp(alog_ref[...]) * dt
    expand = expand_ref[...]

    def widen(v):
        return jnp.dot(v, expand, precision=lax.Precision.HIGHEST, preferred_element_type=F32)

    xc = xs * widen(dt)
    tri = (lax.broadcasted_iota(jnp.int32, (SSD_CHUNK, SSD_CHUNK), 0)
           >= lax.broadcasted_iota(jnp.int32, (SSD_CHUNK, SSD_CHUNK), 1))
    hpg = SSD_HEADS // SSD_GROUPS
    gw = hpg * SSD_HEAD_DIM
    y_chunks = []
    for c in range(ts // SSD_CHUNK):
        rows = slice(c * SSD_CHUNK, (c + 1) * SSD_CHUNK)
        acs = adt[rows]
        d = 1
        while d < SSD_CHUNK:
            acs = acs + _shift_rows(acs, d, 0.0)
            d *= 2
        acs_t = acs.T
        total = acs[SSD_CHUNK - 1:SSD_CHUNK, :]
        e_in = widen(jnp.exp(acs))
        e_out = widen(jnp.exp(total - acs))
        e_all = widen(jnp.exp(total) * jnp.ones((SUBLANES, 1), F32))[0:1, :]
        xc_c = xc[rows]
        xdec = (xc_c * e_out).astype(BF16)
        xc_bf = xc_c.astype(BF16)
        y_parts = []
        for g in range(SSD_GROUPS):
            bg = xbc[rows, SSD_WIDTH + g * SSD_STATE:SSD_WIDTH + (g + 1) * SSD_STATE].astype(BF16)
            cg = xbc[rows, SSD_WIDTH + (SSD_GROUPS + g) * SSD_STATE:
                     SSD_WIDTH + (SSD_GROUPS + g + 1) * SSD_STATE].astype(BF16)
            scores = lax.dot_general(cg, bg, (((1,), (1,)), ((), ())), preferred_element_type=F32)
            gs = slice(g * gw, (g + 1) * gw)
            st = state_ref[:, gs]
            y_off = jnp.dot(cg, st.astype(BF16), preferred_element_type=F32) * e_in[:, gs]
            diag = []
            for hh in range(hpg):
                h = g * hpg + hh
                lm = jnp.where(tri, jnp.exp(acs[:, h:h + 1] - acs_t[h:h + 1, :]), 0.0)
                p = (scores * lm).astype(BF16)
                diag.append(jnp.dot(p, xc_bf[:, h * SSD_HEAD_DIM:(h + 1) * SSD_HEAD_DIM],
                                    preferred_element_type=F32))
            y_parts.append(jnp.concatenate(diag, axis=1) + y_off)
            new = lax.dot_general(bg, xdec[:, gs], (((0,), (0,)), ((), ())),
                                  preferred_element_type=F32)
            state_ref[:, gs] = st * e_all[:, gs] + new
        y_chunks.append(jnp.concatenate(y_parts, axis=1))
    y = jnp.concatenate(y_chunks, axis=0) + dexp_ref[...] * xs
    yg = y * jax.nn.silu(z_ref[...])
    y_ref[:, LRU_WIDTH:] = _rms(yg, nw_ref[...]).astype(y_ref.dtype)


def _mixer(proj, bsz, seq, lcw, lcb, wg, ba, bx, lam, scw, scb, dtb, alog, dexp, nw, expand):
    nj = seq // SEQ_TILE

    def col(width, idx):
        return pl.BlockSpec((SEQ_TILE, width), lambda b, j: (b * nj + j, idx))

    def full(arr):
        nd = arr.ndim
        return pl.BlockSpec(arr.shape, lambda b, j: (0,) * nd)

    consts = (lcw, lcb, wg, ba, bx, lam, scw, scb, dtb, alog, dexp, nw, expand)
    return pl.pallas_call(
        _mixer_kernel,
        grid=(bsz, nj),
        in_specs=[col(1024, 0), col(1024, 1), col(1024, 2), col(SSD_CONV_DIM, 2),
                  col(DT_PAD, (3 * 1024 + SSD_CONV_DIM) // DT_PAD)] + [full(c) for c in consts],
        out_specs=pl.BlockSpec((SEQ_TILE, 2 * 1024), lambda b, j: (b * nj + j, 0)),
        out_shape=jax.ShapeDtypeStruct((bsz * seq, 2 * 1024), BF16),
        scratch_shapes=[
            pltpu.VMEM((SUBLANES, LRU_WIDTH), F32),
            pltpu.VMEM((SUBLANES, SSD_CONV_DIM), F32),
            pltpu.VMEM((SUBLANES, LRU_WIDTH), F32),
            pltpu.VMEM((SSD_STATE, SSD_WIDTH), F32),
        ],
        compiler_params=pltpu.CompilerParams(
            dimension_semantics=("arbitrary", "arbitrary"), vmem_limit_bytes=VMEM_LIMIT),
        name="mixer",
    )(proj, proj, proj, proj, proj, *consts)


def _sort_desc(v):
    v = list(v)
    n = len(v)
    k = 2
    while k <= n:
        j = k // 2
        while j >= 1:
            for i in range(n):
                l = i ^ j
                if l > i:
                    hi, lo = jnp.maximum(v[i], v[l]), jnp.minimum(v[i], v[l])
                    v[i], v[l] = (hi, lo) if (i & k) == 0 else (lo, hi)
            j //= 2
        k *= 2
    return v


def _merge_top(a, b):
    n = len(a)
    v = [jnp.maximum(a[i], b[n - 1 - i]) for i in range(n)]
    j = n // 2
    while j >= 1:
        for i in range(n):
            l = i ^ j
            if l > i:
                v[i], v[l] = jnp.maximum(v[i], v[l]), jnp.minimum(v[i], v[l])
        j //= 2
    return v


def _top_sorted(vals):
    groups = [_sort_desc(vals[i:i + PEER_TOPK]) for i in range(0, len(vals), PEER_TOPK)]
    while len(groups) > 1:
        nxt = [_merge_top(groups[i], groups[i + 1]) for i in range(0, len(groups) - 1, 2)]
        if len(groups) % 2:
            nxt.append(groups[-1])
        groups = nxt
    return groups[0]


def _scores_kernel(y_ref, x_ref, wout_ref, nw_ref, wq_ref, k1_ref, k2_ref,
                   x1_ref, hnt_ref, s1_ref, s2_ref):
    x1 = x_ref[...] + jnp.dot(y_ref[...], wout_ref[...], preferred_element_type=F32)
    x1_ref[...] = x1
    hn = _rms(x1, nw_ref[...]).astype(BF16)
    hnt_ref[...] = hn.T
    q = jnp.dot(hn, wq_ref[...], preferred_element_type=F32).astype(BF16)
    half = PEER_HEADS * PEER_HALF
    nt = (((1,), (1,)), ((), ()))
    s1_ref[...] = lax.dot_general(k1_ref[...], q[:, :half], nt, preferred_element_type=F32)
    s2_ref[...] = lax.dot_general(k2_ref[...], q[:, half:], nt, preferred_element_type=F32)


def _scores(ycat, x2, w_out, nw, wq, k1, k2):
    t = x2.shape[0]
    nrow = PEER_HEADS * PEER_N_KEYS
    tok = pl.BlockSpec((SCORE_TILE, D_MODEL), lambda i: (i, 0))
    tab = pl.BlockSpec((nrow, SCORE_TILE), lambda i: (0, i))

    def full(arr):
        nd = arr.ndim
        return pl.BlockSpec(arr.shape, lambda i: (0,) * nd, pipeline_mode=pl.Buffered(1))

    return pl.pallas_call(
        _scores_kernel,
        grid=(t // SCORE_TILE,),
        in_specs=[tok, tok, full(w_out), full(nw), full(wq), full(k1), full(k2)],
        out_specs=[tok, pl.BlockSpec((D_MODEL, SCORE_TILE), lambda i: (0, i)), tab, tab],
        out_shape=[
            jax.ShapeDtypeStruct((t, D_MODEL), F32),
            jax.ShapeDtypeStruct((D_MODEL, t), BF16),
            jax.ShapeDtypeStruct((nrow, t), F32),
            jax.ShapeDtypeStruct((nrow, t), F32),
        ],
        compiler_params=pltpu.CompilerParams(
            dimension_semantics=("arbitrary",), vmem_limit_bytes=VMEM_LIMIT),
        name="scores",
    )(ycat, x2, w_out, nw, wq, k1, k2)


def _router_kernel(s1_ref, s2_ref, cnt_ref, a1_ref, rank_ref, a2_ref):
    tq = s1_ref.shape[1]
    s1 = s1_ref[...].reshape(PEER_N_KEYS, PEER_HEADS, tq)
    s2 = s2_ref[...]

    a = _top_sorted([s1[n] for n in range(PEER_N_KEYS)])
    b = _top_sorted([s2_ref[pl.ds(n, PEER_HEADS, stride=PEER_N_KEYS), :] for n in range(PEER_N_KEYS)])
    cands = [a[i] + b[j] for i in range(PEER_TOPK) for j in range(PEER_TOPK)
             if (i + 1) * (j + 1) <= PEER_TOPK]
    neg = jnp.full_like(a[0], -jnp.inf)
    tau = _top_sorted(cands + [neg] * (-len(cands) % PEER_TOPK))[PEER_TOPK - 1]
    m = a[0] + b[0]
    z = jnp.zeros_like(m)
    for c in cands:
        z = z + jnp.where(c >= tau, jnp.exp(c - m), 0.0)
    inv_z = 1.0 / z

    cnt = jnp.zeros_like(s1)
    for j in range(PEER_TOPK):
        cnt = cnt + jnp.where(s1 + b[j] >= tau, 1.0, 0.0)
    cnt_ref[...] = cnt.reshape(PEER_N_KEYS * PEER_HEADS, tq)
    a1_ref[...] = jnp.exp(s1 - a[0]).reshape(PEER_N_KEYS * PEER_HEADS, tq)
    for h in range(PEER_HEADS):
        rows = slice(h * PEER_N_KEYS, (h + 1) * PEER_N_KEYS)
        s2h = s2[rows, :]
        rank = jnp.zeros_like(s2h)
        for k in range(PEER_TOPK):
            rank = rank + jnp.where(b[k][h:h + 1, :] > s2h, 1.0, 0.0)
        rank_ref[rows, :] = rank
        a2_ref[rows, :] = jnp.exp(s2h - b[0][h:h + 1, :]) * inv_z[h:h + 1, :]


def _router(s1, s2):
    nrow, t = s1.shape
    tab = pl.BlockSpec((nrow, ROUTER_TILE), lambda i: (0, i))
    return pl.pallas_call(
        _router_kernel,
        grid=(t // ROUTER_TILE,),
        in_specs=[tab, tab],
        out_specs=[tab, tab, tab, tab],
        out_shape=[jax.ShapeDtypeStruct((nrow, t), F32)] * 4,
        compiler_params=pltpu.CompilerParams(
            dimension_semantics=("arbitrary",), vmem_limit_bytes=VMEM_LIMIT),
        name="router",
    )(s1, s2)


def _peer_kernel(hnt_ref, u_ref, vt_ref, rank_ref, a2_ref, cnt_ref, a1_ref, x1_ref, nw_ref,
                 o_ref, acc_ref):
    e = pl.program_id(1)

    @pl.when(e == 0)
    def _():
        acc_ref[...] = jnp.zeros_like(acc_ref)

    act = jax.nn.gelu(jnp.dot(u_ref[...], hnt_ref[...], preferred_element_type=F32))
    parts = []
    for j in range(PEER_TE // PEER_N_KEYS):
        w = None
        for h in range(PEER_HEADS):
            r = j * PEER_HEADS + h
            rows = slice(h * PEER_N_KEYS, (h + 1) * PEER_N_KEYS)
            term = jnp.where(rank_ref[rows, :] < cnt_ref[r:r + 1, :],
                             a2_ref[rows, :] * a1_ref[r:r + 1, :], 0.0)
            w = term if w is None else w + term
        parts.append(w)
    wact = (act * jnp.concatenate(parts, axis=0)).astype(BF16)
    acc_ref[...] += jnp.dot(vt_ref[...], wact, preferred_element_type=F32)

    @pl.when(e == pl.num_programs(1) - 1)
    def _():
        o_ref[...] = _rms(x1_ref[...] + acc_ref[...].T, nw_ref[...])


def _peer(hnt, u_bf, vt_bf, rank, a2, cnt, a1, x1, nw):
    t = x1.shape[0]
    n_exp = u_bf.shape[0]
    nrow = PEER_HEADS * PEER_N_KEYS
    rows_per_step = PEER_HEADS * (PEER_TE // PEER_N_KEYS)
    tok_tab = pl.BlockSpec((nrow, PEER_TQ), lambda i, e: (0, i))
    exp_tab = pl.BlockSpec((rows_per_step, PEER_TQ), lambda i, e: (e, i))
    tok = pl.BlockSpec((PEER_TQ, D_MODEL), lambda i, e: (i, 0))
    return pl.pallas_call(
        _peer_kernel,
        grid=(t // PEER_TQ, n_exp // PEER_TE),
        in_specs=[
            pl.BlockSpec((D_MODEL, PEER_TQ), lambda i, e: (0, i)),
            pl.BlockSpec((PEER_TE, D_MODEL), lambda i, e: (e, 0)),
            pl.BlockSpec((D_MODEL, PEER_TE), lambda i, e: (0, e)),
            tok_tab, tok_tab, exp_tab, exp_tab, tok,
            pl.BlockSpec((1, D_MODEL), lambda i, e: (0, 0)),
        ],
        out_specs=tok,
        out_shape=jax.ShapeDtypeStruct((t, D_MODEL), F32),
        scratch_shapes=[pltpu.VMEM((D_MODEL, PEER_TQ), F32)],
        compiler_params=pltpu.CompilerParams(
            dimension_semantics=("arbitrary", "arbitrary"), vmem_limit_bytes=VMEM_LIMIT),
        name="peer",
    )(hnt, u_bf, vt_bf, rank, a2, cnt, a1, x1, nw)


def _block_diag_chunks(w):
    per = GATE_CHUNK // LRU_HEAD_DIM
    w4 = w.reshape(LRU_HEADS // per, per, LRU_HEAD_DIM, LRU_HEAD_DIM)
    bd = jnp.einsum('cgij,gk->cgikj', w4, jnp.eye(per, dtype=w.dtype))
    return bd.reshape(LRU_HEADS // per, GATE_CHUNK, GATE_CHUNK)


def kernel(x, norm_mix_w, w_in, lru_conv_w, lru_conv_b, lru_wa, lru_ba, lru_wx, lru_bx, lru_lambda, ssd_conv_w, ssd_conv_b, ssd_dt_bias, ssd_a_log, ssd_d, ssd_norm_w, w_out, norm_ffn_w, peer_wq, peer_sub_keys, peer_u, peer_v, norm_final_w):
    bsz, seq, _ = x.shape
    assert norm_mix_w.shape[0] == 1, "one layer"
    t = bsz * seq
    assert seq % SEQ_TILE == 0 and t % PEER_TQ == 0 and t % IN_TILE == 0 and t % SCORE_TILE == 0
    x2 = x.reshape(t, D_MODEL)
    row = lambda v: v.reshape(1, -1).astype(F32)

    w_in_pad = jnp.pad(w_in[0], ((0, 0), (0, PROJ_PAD - w_in.shape[-1]))).astype(BF16)
    wg = jnp.concatenate([_block_diag_chunks(lru_wa[0]), _block_diag_chunks(lru_wx[0])], axis=-1).astype(BF16)
    pad_heads = lambda v: jnp.pad(v.reshape(1, -1).astype(F32), ((0, 0), (0, DT_PAD - SSD_HEADS)))
    expand = (jnp.arange(DT_PAD)[:, None] == (jnp.arange(SSD_WIDTH) // SSD_HEAD_DIM)[None, :]).astype(F32)
    dexp = jnp.repeat(ssd_d[0].astype(F32), SSD_HEAD_DIM).reshape(1, -1)
    wq = peer_wq[0].reshape(D_MODEL, PEER_HEADS, 2, PEER_HALF).transpose(0, 2, 1, 3)
    wq = wq.reshape(D_MODEL, 2 * PEER_HEADS * PEER_HALF).astype(BF16)
    keys = peer_sub_keys[0]
    eye = jnp.eye(PEER_HEADS, dtype=keys.dtype)
    k1 = jnp.einsum('hnd,hg->nhgd', keys[:, 0], eye).reshape(PEER_N_KEYS * PEER_HEADS, -1).astype(BF16)
    k2 = jnp.einsum('hnd,hg->hngd', keys[:, 1], eye).reshape(PEER_HEADS * PEER_N_KEYS, -1).astype(BF16)
    u_bf = peer_u[0].astype(BF16)
    vt_bf = peer_v[0].T.astype(BF16)

    proj = _in_proj(x2, row(norm_mix_w[0]), w_in_pad)
    ycat = _mixer(proj, bsz, seq,
                  lru_conv_w[0].astype(F32), row(lru_conv_b[0]), wg, row(lru_ba[0]), row(lru_bx[0]),
                  row(lru_lambda[0]), ssd_conv_w[0].astype(F32), row(ssd_conv_b[0]),
                  pad_heads(ssd_dt_bias[0]), pad_heads(ssd_a_log[0]), dexp, row(ssd_norm_w[0]), expand)
    x1, hnt, s1, s2 = _scores(ycat, x2, w_out[0].astype(BF16), row(norm_ffn_w[0]), wq, k1, k2)
    cnt, a1, rank, a2 = _router(s1, s2)
    out = _peer(hnt, u_bf, vt_bf, rank, a2, cnt, a1, x1, row(norm_final_w))
    return out.reshape(bsz, seq, D_MODEL)
```

```python
import functools
import math

import jax
import jax.numpy as jnp
from jax import lax
from jax.experimental import pallas as pl
from jax.experimental.pallas import tpu as pltpu

F32 = jnp.float32
BF16 = jnp.bfloat16

D_MODEL = 2048
CONV_WIDTH = 4
LRU_WIDTH = 1024
LRU_HEADS = 16
LRU_HEAD_DIM = 64
LRU_C = 8.0
SSD_WIDTH = 1024
SSD_HEAD_DIM = 64
SSD_HEADS = 16
SSD_GROUPS = 2
SSD_STATE = 128
SSD_CONV_DIM = SSD_WIDTH + 2 * SSD_GROUPS * SSD_STATE
PEER_HEADS = 8
PEER_N_KEYS = 128
PEER_HALF = 128
PEER_TOPK = 16
EPS = 1e-6

LANES = 128
SUBLANES = 8
MXU_WIDTH = 256
DT_PAD = LANES
PROJ_PAD = 3 * 1024 + SSD_CONV_DIM + DT_PAD
GATE_CHUNK = 256

IN_TILE = 256
SEQ_TILE = 256
SSD_CHUNK = 128
SCORE_TILE = 256
ROUTER_TILE = 128
PEER_TQ = 512
PEER_TE = 512
PEER_SKEW = 1
VMEM_LIMIT = 56 * 1024 * 1024


def _rms(x, w):
    return x * lax.rsqrt(jnp.mean(x * x, axis=-1, keepdims=True) + EPS) * w


def _in_proj_kernel(x_ref, nw_ref, w_ref, o_ref):
    hn = _rms(x_ref[...], nw_ref[...])
    o_ref[...] = jnp.dot(hn.astype(BF16), w_ref[...], preferred_element_type=F32)


def _in_proj(x2, norm_w, w_in_pad):
    t = x2.shape[0]
    return pl.pallas_call(
        _in_proj_kernel,
        grid=(t // IN_TILE,),
        in_specs=[
            pl.BlockSpec((IN_TILE, D_MODEL), lambda i: (i, 0)),
            pl.BlockSpec((1, D_MODEL), lambda i: (0, 0)),
            pl.BlockSpec((D_MODEL, PROJ_PAD), lambda i: (0, 0), pipeline_mode=pl.Buffered(1)),
        ],
        out_specs=pl.BlockSpec((IN_TILE, PROJ_PAD), lambda i: (i, 0)),
        out_shape=jax.ShapeDtypeStruct((t, PROJ_PAD), F32),
        compiler_params=pltpu.CompilerParams(
            dimension_semantics=("arbitrary",), vmem_limit_bytes=VMEM_LIMIT),
        name="in_proj",
    )(x2, norm_w, w_in_pad)


def _shift_rows(x, d, fill):
    rolled = pltpu.roll(x, d, axis=0)
    row = lax.broadcasted_iota(jnp.int32, x.shape, 0)
    return jnp.where(row < d, fill, rolled)


def _causal_conv(cur, prev8, w_ref, b_ref):
    row8 = lax.broadcasted_iota(jnp.int32, prev8.shape, 0)
    out = cur * w_ref[CONV_WIDTH - 1:CONV_WIDTH, :] + b_ref[...]
    for d in range(1, CONV_WIDTH):
        rolled = pltpu.roll(cur, d, axis=0)
        top = jnp.where(row8 < d, pltpu.roll(prev8, d, axis=0), rolled[:SUBLANES])
        shifted = jnp.concatenate([top, rolled[SUBLANES:]], axis=0)
        k = CONV_WIDTH - 1 - d
        out = out + shifted * w_ref[k:k + 1, :]
    return out


def _mixer_kernel(lx_ref, lg_ref, z_ref, xbc_ref, dt_ref,
                  lcw_ref, lcb_ref, wg_ref, ba_ref, bx_ref, lam_ref,
                  scw_ref, scb_ref, dtb_ref, alog_ref, dexp_ref, nw_ref, expand_ref,
                  y_ref,
                  lprev_ref, sprev_ref, h_ref, state_ref):
    ts = lx_ref.shape[0]

    @pl.when(pl.program_id(1) == 0)
    def _():
        lprev_ref[...] = jnp.zeros_like(lprev_ref)
        sprev_ref[...] = jnp.zeros_like(sprev_ref)
        h_ref[...] = jnp.zeros_like(h_ref)
        state_ref[...] = jnp.zeros_like(state_ref)

    lx = lx_ref[...]
    xl = _causal_conv(lx, lprev_ref[...], lcw_ref, lcb_ref)
    lprev_ref[...] = lx[ts - SUBLANES:, :]
    xl_bf = xl.astype(BF16)
    ra, ri = [], []
    for c in range(LRU_WIDTH // GATE_CHUNK):
        g = jnp.dot(xl_bf[:, c * GATE_CHUNK:(c + 1) * GATE_CHUNK], wg_ref[c],
                    preferred_element_type=F32)
        ra.append(g[:, :GATE_CHUNK])
        ri.append(g[:, GATE_CHUNK:])
    r = jax.nn.sigmoid(jnp.concatenate(ra, axis=1) + ba_ref[...])
    i = jax.nn.sigmoid(jnp.concatenate(ri, axis=1) + bx_ref[...])
    log_a = -LRU_C * r * jax.nn.softplus(-lam_ref[...])
    a = jnp.exp(log_a)
    b = jnp.sqrt(jnp.tanh(-log_a) * (1.0 + a * a)) * (i * xl)
    d = 1
    while d < ts:
        b = a * _shift_rows(b, d, 0.0) + b
        a = a * _shift_rows(a, d, 1.0)
        d *= 2
    hs = a * h_ref[0:1, :] + b
    h_ref[...] = jnp.broadcast_to(hs[ts - 1:ts, :], h_ref.shape)
    y_ref[:, :LRU_WIDTH] = (hs * jax.nn.gelu(lg_ref[...])).astype(y_ref.dtype)

    xbc_raw = xbc_ref[...]
    xbc = jax.nn.silu(_causal_conv(xbc_raw, sprev_ref[...], scw_ref, scb_ref))
    sprev_ref[...] = xbc_raw[ts - SUBLANES:, :]
    xs = xbc[:, :SSD_WIDTH]
    dt = jax.nn.softplus(dt_ref[...] + dtb_ref[...])
    adt = -jnp.exp(alog_ref[...]) * dt
    expand = expand_ref[...]

    def widen(v):
        return jnp.dot(v, expand, precision=lax.Precision.HIGHEST, preferred_element_type=F32)

    xc = xs * widen(dt)
    tri = (lax.broadcasted_iota(jnp.int32, (SSD_CHUNK, SSD_CHUNK), 0)
           >= lax.broadcasted_iota(jnp.int32, (SSD_CHUNK, SSD_CHUNK), 1))
    hpg = SSD_HEADS // SSD_GROUPS
    gw = hpg * SSD_HEAD_DIM
    y_chunks = []
    for c in range(ts // SSD_CHUNK):
        rows = slice(c * SSD_CHUNK, (c + 1) * SSD_CHUNK)
        acs = adt[rows]
        d = 1
        while d < SSD_CHUNK:
            acs = acs + _shift_rows(acs, d, 0.0)
            d *= 2
        acs_t = acs.T
        total = acs[SSD_CHUNK - 1:SSD_CHUNK, :]
        e_in = widen(jnp.exp(acs))
        e_out = widen(jnp.exp(total - acs))
        e_all = widen(jnp.exp(total) * jnp.ones((SUBLANES, 1), F32))[0:1, :]
        xc_c = xc[rows]
        xdec = (xc_c * e_out).astype(BF16)
        xc_bf = xc_c.astype(BF16)
        y_parts = []
        for g in range(SSD_GROUPS):
            bg = xbc[rows, SSD_WIDTH + g * SSD_STATE:SSD_WIDTH + (g + 1) * SSD_STATE].astype(BF16)
            cg = xbc[rows, SSD_WIDTH + (SSD_GROUPS + g) * SSD_STATE:
                     SSD_WIDTH + (SSD_GROUPS + g + 1) * SSD_STATE].astype(BF16)
            scores = lax.dot_general(cg, bg, (((1,), (1,)), ((), ())), preferred_element_type=F32)
            gs = slice(g * gw, (g + 1) * gw)
            st = state_ref[:, gs]
            y_off = jnp.dot(cg, st.astype(BF16), preferred_element_type=F32) * e_in[:, gs]
            diag = []
            for hh in range(hpg):
                h = g * hpg + hh
                lm = jnp.where(tri, jnp.exp(acs[:, h:h + 1] - acs_t[h:h + 1, :]), 0.0)
                p = (scores * lm).astype(BF16)
                diag.append(jnp.dot(p, xc_bf[:, h * SSD_HEAD_DIM:(h + 1) * SSD_HEAD_DIM],
                                    preferred_element_type=F32))
            y_parts.append(jnp.concatenate(diag, axis=1) + y_off)
            new = lax.dot_general(bg, xdec[:, gs], (((0,), (0,)), ((), ())),
                                  preferred_element_type=F32)
            state_ref[:, gs] = st * e_all[:, gs] + new
        y_chunks.append(jnp.concatenate(y_parts, axis=1))
    y = jnp.concatenate(y_chunks, axis=0) + dexp_ref[...] * xs
    yg = y * jax.nn.silu(z_ref[...])
    y_ref[:, LRU_WIDTH:] = _rms(yg, nw_ref[...]).astype(y_ref.dtype)


def _mixer(proj, bsz, seq, lcw, lcb, wg, ba, bx, lam, scw, scb, dtb, alog, dexp, nw, expand):
    nj = seq // SEQ_TILE

    def col(width, idx):
        return pl.BlockSpec((SEQ_TILE, width), lambda b, j: (b * nj + j, idx))

    def full(arr):
        nd = arr.ndim
        return pl.BlockSpec(arr.shape, lambda b, j: (0,) * nd)

    consts = (lcw, lcb, wg, ba, bx, lam, scw, scb, dtb, alog, dexp, nw, expand)
    return pl.pallas_call(
        _mixer_kernel,
        grid=(bsz, nj),
        in_specs=[col(1024, 0), col(1024, 1), col(1024, 2), col(SSD_CONV_DIM, 2),
                  col(DT_PAD, (3 * 1024 + SSD_CONV_DIM) // DT_PAD)] + [full(c) for c in consts],
        out_specs=pl.BlockSpec((SEQ_TILE, 2 * 1024), lambda b, j: (b * nj + j, 0)),
        out_shape=jax.ShapeDtypeStruct((bsz * seq, 2 * 1024), BF16),
        scratch_shapes=[
            pltpu.VMEM((SUBLANES, LRU_WIDTH), F32),
            pltpu.VMEM((SUBLANES, SSD_CONV_DIM), F32),
            pltpu.VMEM((SUBLANES, LRU_WIDTH), F32),
            pltpu.VMEM((SSD_STATE, SSD_WIDTH), F32),
        ],
        compiler_params=pltpu.CompilerParams(
            dimension_semantics=("arbitrary", "arbitrary"), vmem_limit_bytes=VMEM_LIMIT),
        name="mixer",
    )(proj, proj, proj, proj, proj, *consts)


def _sort_desc(v):
    v = list(v)
    n = len(v)
    k = 2
    while k <= n:
        j = k // 2
        while j >= 1:
            for i in range(n):
                l = i ^ j
                if l > i:
                    hi, lo = jnp.maximum(v[i], v[l]), jnp.minimum(v[i], v[l])
                    v[i], v[l] = (hi, lo) if (i & k) == 0 else (lo, hi)
            j //= 2
        k *= 2
    return v


def _merge_top(a, b):
    n = len(a)
    v = [jnp.maximum(a[i], b[n - 1 - i]) for i in range(n)]
    j = n // 2
    while j >= 1:
        for i in range(n):
            l = i ^ j
            if l > i:
                v[i], v[l] = jnp.maximum(v[i], v[l]), jnp.minimum(v[i], v[l])
        j //= 2
    return v


def _top_sorted(vals):
    groups = [_sort_desc(vals[i:i + PEER_TOPK]) for i in range(0, len(vals), PEER_TOPK)]
    while len(groups) > 1:
        nxt = [_merge_top(groups[i], groups[i + 1]) for i in range(0, len(groups) - 1, 2)]
        if len(groups) % 2:
            nxt.append(groups[-1])
        groups = nxt
    return groups[0]


def _scores_kernel(y_ref, x_ref, wout_ref, nw_ref, wq_ref, k1_ref, k2_ref,
                   x1_ref, hnt_ref, s1_ref, s2_ref):
    x1 = x_ref[...] + jnp.dot(y_ref[...], wout_ref[...], preferred_element_type=F32)
    x1_ref[...] = x1
    hn = _rms(x1, nw_ref[...]).astype(BF16)
    hnt_ref[...] = hn.T
    q = jnp.dot(hn, wq_ref[...], preferred_element_type=F32).astype(BF16)
    half = PEER_HEADS * PEER_HALF
    nt = (((1,), (1,)), ((), ()))
    s1_ref[...] = lax.dot_general(k1_ref[...], q[:, :half], nt, preferred_element_type=F32)
    s2_ref[...] = lax.dot_general(k2_ref[...], q[:, half:], nt, preferred_element_type=F32)


def _scores(ycat, x2, w_out, nw, wq, k1, k2):
    t = x2.shape[0]
    nrow = PEER_HEADS * PEER_N_KEYS
    tok = pl.BlockSpec((SCORE_TILE, D_MODEL), lambda i: (i, 0))
    tab = pl.BlockSpec((nrow, SCORE_TILE), lambda i: (0, i))

    def full(arr):
        nd = arr.ndim
        return pl.BlockSpec(arr.shape, lambda i: (0,) * nd, pipeline_mode=pl.Buffered(1))

    return pl.pallas_call(
        _scores_kernel,
        grid=(t // SCORE_TILE,),
        in_specs=[tok, tok, full(w_out), full(nw), full(wq), full(k1), full(k2)],
        out_specs=[tok, pl.BlockSpec((D_MODEL, SCORE_TILE), lambda i: (0, i)), tab, tab],
        out_shape=[
            jax.ShapeDtypeStruct((t, D_MODEL), F32),
            jax.ShapeDtypeStruct((D_MODEL, t), BF16),
            jax.ShapeDtypeStruct((nrow, t), F32),
            jax.ShapeDtypeStruct((nrow, t), F32),
        ],
        compiler_params=pltpu.CompilerParams(
            dimension_semantics=("arbitrary",), vmem_limit_bytes=VMEM_LIMIT),
        name="scores",
    )(ycat, x2, w_out, nw, wq, k1, k2)


def _router_kernel(s1_ref, s2_ref, cnt_ref, a1_ref, rank_ref, a2_ref):
    tq = s1_ref.shape[1]
    s1 = s1_ref[...].reshape(PEER_N_KEYS, PEER_HEADS, tq)
    s2 = s2_ref[...]

    a = _top_sorted([s1[n] for n in range(PEER_N_KEYS)])
    b = _top_sorted([s2_ref[pl.ds(n, PEER_HEADS, stride=PEER_N_KEYS), :] for n in range(PEER_N_KEYS)])
    cands = [a[i] + b[j] for i in range(PEER_TOPK) for j in range(PEER_TOPK)
             if (i + 1) * (j + 1) <= PEER_TOPK]
    neg = jnp.full_like(a[0], -jnp.inf)
    tau = _top_sorted(cands + [neg] * (-len(cands) % PEER_TOPK))[PEER_TOPK - 1]
    m = a[0] + b[0]
    z = jnp.zeros_like(m)
    for c in cands:
        z = z + jnp.where(c >= tau, jnp.exp(c - m), 0.0)
    inv_z = 1.0 / z

    cnt = jnp.zeros_like(s1)
    for j in range(PEER_TOPK):
        cnt = cnt + jnp.where(s1 + b[j] >= tau, 1.0, 0.0)
    cnt_ref[...] = cnt.reshape(PEER_N_KEYS * PEER_HEADS, tq)
    a1_ref[...] = jnp.exp(s1 - a[0]).reshape(PEER_N_KEYS * PEER_HEADS, tq)
    for h in range(PEER_HEADS):
        rows = slice(h * PEER_N_KEYS, (h + 1) * PEER_N_KEYS)
        s2h = s2[rows, :]
        rank = jnp.zeros_like(s2h)
        for k in range(PEER_TOPK):
            rank = rank + jnp.where(b[k][h:h + 1, :] > s2h, 1.0, 0.0)
        rank_ref[rows, :] = rank.astype(rank_ref.dtype)
        a2_ref[rows, :] = (jnp.exp(s2h - b[0][h:h + 1, :]) * inv_z[h:h + 1, :]).astype(a2_ref.dtype)


def _router(s1, s2):
    nrow, t = s1.shape
    tab = pl.BlockSpec((nrow, ROUTER_TILE), lambda i: (0, i))
    return pl.pallas_call(
        _router_kernel,
        grid=(t // ROUTER_TILE,),
        in_specs=[tab, tab],
        out_specs=[tab, tab, tab, tab],
        out_shape=[jax.ShapeDtypeStruct((nrow, t), F32)] * 4,
        compiler_params=pltpu.CompilerParams(
            dimension_semantics=("arbitrary",), vmem_limit_bytes=VMEM_LIMIT),
        name="router",
    )(s1, s2)


def _peer_kernel(hnt_ref, u_ref, vt_ref, rank32_ref, a232_ref, cnt_ref, a1_ref, x1_ref, nw_ref,
                 o_ref, acc_ref, act0_ref, act1_ref, cntb_ref, a1b_ref, rank_ref, a2_ref):
    s = pl.program_id(1)
    n_steps = pl.num_programs(1)

    @pl.when((pl.program_id(0) == 0) & (s == 0))
    def _():
        act1_ref[...] = jnp.zeros_like(act1_ref)
        acc_ref[...] = jnp.zeros_like(acc_ref)

    @pl.when(s == PEER_SKEW)
    def _():
        acc_ref[...] = jnp.zeros_like(acc_ref)

    @pl.when(s == 0)
    def _():
        rank_ref[...] = rank32_ref[...].astype(BF16)
        a2_ref[...] = a232_ref[...].astype(BF16)

    pack = 2 * SUBLANES

    def bf16_rows(ref, r, lanes):
        return pltpu.repeat(ref[r * pack:(r + 1) * pack, lanes], PEER_N_KEYS // pack, axis=0)

    def gated(act_r, j, c):
        lanes = slice(c * LANES, (c + 1) * LANES)
        w = None
        for h in range(PEER_HEADS):
            r = j * PEER_HEADS + h
            rows = slice(h * PEER_N_KEYS, (h + 1) * PEER_N_KEYS)
            term = jnp.where(rank_ref[rows, lanes] < bf16_rows(cntb_ref, r, lanes),
                             a2_ref[rows, lanes] * bf16_rows(a1b_ref, r, lanes),
                             jnp.zeros((), BF16))
            w = term if w is None else w + term
        return jax.nn.gelu(act_r[j * PEER_N_KEYS:(j + 1) * PEER_N_KEYS, lanes]).astype(BF16) * w

    def stages(act_r, act_w):
        for src, dst in ((cnt_ref, cntb_ref), (a1_ref, a1b_ref)):
            for r in range(src.shape[0]):
                dst[r * pack:(r + 1) * pack, :] = jnp.broadcast_to(
                    src[r:r + 1, :], (pack, PEER_TQ)).astype(BF16)
        per = MXU_WIDTH // LANES
        for p in range(PEER_TQ // MXU_WIDTH):
            lanes = slice(p * MXU_WIDTH, (p + 1) * MXU_WIDTH)
            wact = jnp.concatenate(
                [jnp.concatenate([gated(act_r, j, p * per + cc) for cc in range(per)], axis=1)
                 for j in range(PEER_TE // PEER_N_KEYS)], axis=0)
            act_w[:, lanes] = jnp.dot(u_ref[...], hnt_ref[:, lanes], preferred_element_type=F32)
            acc_ref[:, lanes] += jnp.dot(vt_ref[...], wact, preferred_element_type=F32)

    @pl.when(s % 2 == 0)
    def _():
        stages(act1_ref, act0_ref)

    @pl.when(s % 2 == 1)
    def _():
        stages(act0_ref, act1_ref)

    @pl.when(s == n_steps - 1)
    def _():
        o_ref[...] = _rms(x1_ref[...] + acc_ref[...].T, nw_ref[...])


def _peer(hnt, u_bf, vt_bf, rank, a2, cnt, a1, x1, nw):
    t = x1.shape[0]
    n_exp = u_bf.shape[0]
    nrow = PEER_HEADS * PEER_N_KEYS
    rows_per_step = PEER_HEADS * (PEER_TE // PEER_N_KEYS)
    n_tiles = n_exp // PEER_TE
    tile = lambda s, lag: jnp.clip(s - lag, 0, n_tiles - 1)
    tok_tab = pl.BlockSpec((nrow, PEER_TQ), lambda i, s: (0, i))
    exp_tab = pl.BlockSpec((rows_per_step, PEER_TQ), lambda i, s: (tile(s, PEER_SKEW), i))
    tok = pl.BlockSpec((PEER_TQ, D_MODEL), lambda i, s: (i, 0))
    return pl.pallas_call(
        _peer_kernel,
        grid=(t // PEER_TQ, n_tiles + PEER_SKEW),
        in_specs=[
            pl.BlockSpec((D_MODEL, PEER_TQ), lambda i, s: (0, i)),
            pl.BlockSpec((PEER_TE, D_MODEL), lambda i, s: (tile(s, 0), 0)),
            pl.BlockSpec((D_MODEL, PEER_TE), lambda i, s: (0, tile(s, PEER_SKEW))),
            tok_tab, tok_tab, exp_tab, exp_tab, tok,
            pl.BlockSpec((1, D_MODEL), lambda i, s: (0, 0)),
        ],
        out_specs=tok,
        out_shape=jax.ShapeDtypeStruct((t, D_MODEL), F32),
        scratch_shapes=[pltpu.VMEM((D_MODEL, PEER_TQ), F32),
                        pltpu.VMEM((PEER_TE, PEER_TQ), F32),
                        pltpu.VMEM((PEER_TE, PEER_TQ), F32),
                        pltpu.VMEM((rows_per_step * 2 * SUBLANES, PEER_TQ), BF16),
                        pltpu.VMEM((rows_per_step * 2 * SUBLANES, PEER_TQ), BF16),
                        pltpu.VMEM((nrow, PEER_TQ), BF16),
                        pltpu.VMEM((nrow, PEER_TQ), BF16)],
        compiler_params=pltpu.CompilerParams(
            dimension_semantics=("arbitrary", "arbitrary"), vmem_limit_bytes=VMEM_LIMIT),
        name="peer",
    )(hnt, u_bf, vt_bf, rank, a2, cnt, a1, x1, nw)


def _block_diag_chunks(w):
    per = GATE_CHUNK // LRU_HEAD_DIM
    w4 = w.reshape(LRU_HEADS // per, per, LRU_HEAD_DIM, LRU_HEAD_DIM)
    bd = jnp.einsum('cgij,gk->cgikj', w4, jnp.eye(per, dtype=w.dtype))
    return bd.reshape(LRU_HEADS // per, GATE_CHUNK, GATE_CHUNK)


def kernel(x, norm_mix_w, w_in, lru_conv_w, lru_conv_b, lru_wa, lru_ba, lru_wx, lru_bx, lru_lambda, ssd_conv_w, ssd_conv_b, ssd_dt_bias, ssd_a_log, ssd_d, ssd_norm_w, w_out, norm_ffn_w, peer_wq, peer_sub_keys, peer_u, peer_v, norm_final_w):
    bsz, seq, _ = x.shape
    assert norm_mix_w.shape[0] == 1, "one layer"
    t = bsz * seq
    assert seq % SEQ_TILE == 0 and t % PEER_TQ == 0 and t % IN_TILE == 0 and t % SCORE_TILE == 0
    x2 = x.reshape(t, D_MODEL)
    row = lambda v: v.reshape(1, -1).astype(F32)

    w_in_pad = jnp.pad(w_in[0], ((0, 0), (0, PROJ_PAD - w_in.shape[-1]))).astype(BF16)
    wg = jnp.concatenate([_block_diag_chunks(lru_wa[0]), _block_diag_chunks(lru_wx[0])], axis=-1).astype(BF16)
    pad_heads = lambda v: jnp.pad(v.reshape(1, -1).astype(F32), ((0, 0), (0, DT_PAD - SSD_HEADS)))
    expand = (jnp.arange(DT_PAD)[:, None] == (jnp.arange(SSD_WIDTH) // SSD_HEAD_DIM)[None, :]).astype(F32)
    dexp = jnp.repeat(ssd_d[0].astype(F32), SSD_HEAD_DIM).reshape(1, -1)
    wq = peer_wq[0].reshape(D_MODEL, PEER_HEADS, 2, PEER_HALF).transpose(0, 2, 1, 3)
    wq = wq.reshape(D_MODEL, 2 * PEER_HEADS * PEER_HALF).astype(BF16)
    keys = peer_sub_keys[0]
    eye = jnp.eye(PEER_HEADS, dtype=keys.dtype)
    k1 = jnp.einsum('hnd,hg->nhgd', keys[:, 0], eye).reshape(PEER_N_KEYS * PEER_HEADS, -1).astype(BF16)
    k2 = jnp.einsum('hnd,hg->hngd', keys[:, 1], eye).reshape(PEER_HEADS * PEER_N_KEYS, -1).astype(BF16)
    u_bf = peer_u[0].astype(BF16)
    vt_bf = peer_v[0].T.astype(BF16)

    proj = _in_proj(x2, row(norm_mix_w[0]), w_in_pad)
    ycat = _mixer(proj, bsz, seq,
                  lru_conv_w[0].astype(F32), row(lru_conv_b[0]), wg, row(lru_ba[0]), row(lru_bx[0]),
                  row(lru_lambda[0]), ssd_conv_w[0].astype(F32), row(ssd_conv_b[0]),
                  pad_heads(ssd_dt_bias[0]), pad_heads(ssd_a_log[0]), dexp, row(ssd_norm_w[0]), expand)
    x1, hnt, s1, s2 = _scores(ycat, x2, w_out[0].astype(BF16), row(norm_ffn_w[0]), wq, k1, k2)
    cnt, a1, rank, a2 = _router(s1, s2)
    out = _peer(hnt, u_bf, vt_bf, rank, a2, cnt, a1, x1, row(norm_final_w))
    return out.reshape(bsz, seq, D_MODEL)
```

```python
import functools
import math

import jax
import jax.numpy as jnp
from jax import lax
from jax.experimental import pallas as pl
from jax.experimental.pallas import tpu as pltpu

F32 = jnp.float32
BF16 = jnp.bfloat16

D_MODEL = 2048
CONV_WIDTH = 4
LRU_WIDTH = 1024
LRU_HEADS = 16
LRU_HEAD_DIM = 64
LRU_C = 8.0
SSD_WIDTH = 1024
SSD_HEAD_DIM = 64
SSD_HEADS = 16
SSD_GROUPS = 2
SSD_STATE = 128
SSD_CONV_DIM = SSD_WIDTH + 2 * SSD_GROUPS * SSD_STATE
PEER_HEADS = 8
PEER_N_KEYS = 128
PEER_HALF = 128
PEER_TOPK = 16
EPS = 1e-6

LANES = 128
SUBLANES = 8
MXU_WIDTH = 256
DT_PAD = LANES
PROJ_PAD = 3 * 1024 + SSD_CONV_DIM + DT_PAD
GATE_CHUNK = 256

IN_TILE = 256
SEQ_TILE = 256
SSD_CHUNK = 128
SCORE_TILE = 256
ROUTER_TILE = 128
PEER_TQ = 512
PEER_TE = 512
PEER_SKEW = 1
VMEM_LIMIT = 56 * 1024 * 1024


def _rms(x, w):
    return x * lax.rsqrt(jnp.mean(x * x, axis=-1, keepdims=True) + EPS) * w


def _in_proj_kernel(x_ref, nw_ref, w_ref, o_ref):
    hn = _rms(x_ref[...], nw_ref[...])
    o_ref[...] = jnp.dot(hn.astype(BF16), w_ref[...], preferred_element_type=F32)


def _in_proj(x2, norm_w, w_in_pad):
    t = x2.shape[0]
    return pl.pallas_call(
        _in_proj_kernel,
        grid=(t // IN_TILE,),
        in_specs=[
            pl.BlockSpec((IN_TILE, D_MODEL), lambda i: (i, 0)),
            pl.BlockSpec((1, D_MODEL), lambda i: (0, 0)),
            pl.BlockSpec((D_MODEL, PROJ_PAD), lambda i: (0, 0), pipeline_mode=pl.Buffered(1)),
        ],
        out_specs=pl.BlockSpec((IN_TILE, PROJ_PAD), lambda i: (i, 0)),
        out_shape=jax.ShapeDtypeStruct((t, PROJ_PAD), F32),
        compiler_params=pltpu.CompilerParams(
            dimension_semantics=("arbitrary",), vmem_limit_bytes=VMEM_LIMIT),
        name="in_proj",
    )(x2, norm_w, w_in_pad)


def _shift_rows(x, d, fill):
    rolled = pltpu.roll(x, d, axis=0)
    row = lax.broadcasted_iota(jnp.int32, x.shape, 0)
    return jnp.where(row < d, fill, rolled)


def _causal_conv(cur, prev8, w_ref, b_ref):
    row8 = lax.broadcasted_iota(jnp.int32, prev8.shape, 0)
    out = cur * w_ref[CONV_WIDTH - 1:CONV_WIDTH, :] + b_ref[...]
    for d in range(1, CONV_WIDTH):
        rolled = pltpu.roll(cur, d, axis=0)
        top = jnp.where(row8 < d, pltpu.roll(prev8, d, axis=0), rolled[:SUBLANES])
        shifted = jnp.concatenate([top, rolled[SUBLANES:]], axis=0)
        k = CONV_WIDTH - 1 - d
        out = out + shifted * w_ref[k:k + 1, :]
    return out


def _mixer_kernel(lx_ref, lg_ref, z_ref, xbc_ref, dt_ref,
                  lcw_ref, lcb_ref, wg_ref, ba_ref, bx_ref, lam_ref,
                  scw_ref, scb_ref, dtb_ref, alog_ref, dexp_ref, nw_ref, expand_ref,
                  y_ref,
                  lprev_ref, sprev_ref, h_ref, state_ref):
    ts = lx_ref.shape[0]

    @pl.when(pl.program_id(1) == 0)
    def _():
        lprev_ref[...] = jnp.zeros_like(lprev_ref)
        sprev_ref[...] = jnp.zeros_like(sprev_ref)
        h_ref[...] = jnp.zeros_like(h_ref)
        state_ref[...] = jnp.zeros_like(state_ref)

    lx = lx_ref[...]
    xl = _causal_conv(lx, lprev_ref[...], lcw_ref, lcb_ref)
    lprev_ref[...] = lx[ts - SUBLANES:, :]
    xl_bf = xl.astype(BF16)
    ra, ri = [], []
    for c in range(LRU_WIDTH // GATE_CHUNK):
        g = jnp.dot(xl_bf[:, c * GATE_CHUNK:(c + 1) * GATE_CHUNK], wg_ref[c],
                    preferred_element_type=F32)
        ra.append(g[:, :GATE_CHUNK])
        ri.append(g[:, GATE_CHUNK:])
    r = jax.nn.sigmoid(jnp.concatenate(ra, axis=1) + ba_ref[...])
    i = jax.nn.sigmoid(jnp.concatenate(ri, axis=1) + bx_ref[...])
    log_a = -LRU_C * r * jax.nn.softplus(-lam_ref[...])
    a = jnp.exp(log_a)
    b2 = jnp.tanh(-log_a) * (1.0 + a * a)
    b = jnp.where(b2 > 0.0, b2 * lax.rsqrt(b2), 0.0) * (i * xl)
    in_group = lax.broadcasted_iota(jnp.int32, a.shape, 0) % SUBLANES
    d = 1
    while d < SUBLANES:
        b = a * jnp.where(in_group < d, 0.0, pltpu.roll(b, d, axis=0)) + b
        a = a * jnp.where(in_group < d, 1.0, pltpu.roll(a, d, axis=0))
        d *= 2
    h = h_ref[0:1, :]
    groups = []
    for g in range(ts // SUBLANES):
        rows = slice(g * SUBLANES, (g + 1) * SUBLANES)
        hg = a[rows] * h + b[rows]
        groups.append(hg)
        h = hg[SUBLANES - 1:SUBLANES, :]
    hs = jnp.concatenate(groups, axis=0)
    h_ref[...] = jnp.broadcast_to(h, h_ref.shape)
    y_ref[:, :LRU_WIDTH] = (hs * jax.nn.gelu(lg_ref[...])).astype(y_ref.dtype)

    xbc_raw = xbc_ref[...]
    xbc = jax.nn.silu(_causal_conv(xbc_raw, sprev_ref[...], scw_ref, scb_ref))
    sprev_ref[...] = xbc_raw[ts - SUBLANES:, :]
    xs = xbc[:, :SSD_WIDTH]
    dt = jax.nn.softplus(dt_ref[...] + dtb_ref[...])
    adt = -jnp.exp(alog_ref[...]) * dt
    expand = expand_ref[...]

    def widen(v):
        return jnp.dot(v, expand, precision=lax.Precision.HIGHEST, preferred_element_type=F32)

    xc = xs * widen(dt)
    tri = (lax.broadcasted_iota(jnp.int32, (SSD_CHUNK, SSD_CHUNK), 0)
           >= lax.broadcasted_iota(jnp.int32, (SSD_CHUNK, SSD_CHUNK), 1))
    hpg = SSD_HEADS // SSD_GROUPS
    gw = hpg * SSD_HEAD_DIM
    y_chunks = []
    for c in range(ts // SSD_CHUNK):
        rows = slice(c * SSD_CHUNK, (c + 1) * SSD_CHUNK)
        acs = adt[rows]
        d = 1
        while d < SSD_CHUNK:
            acs = acs + _shift_rows(acs, d, 0.0)
            d *= 2
        acs_t = acs.T
        total = acs[SSD_CHUNK - 1:SSD_CHUNK, :]
        e_in = widen(jnp.exp(acs))
        e_out = widen(jnp.exp(total - acs))
        e_all = widen(jnp.exp(total) * jnp.ones((SUBLANES, 1), F32))[0:1, :]
        xc_c = xc[rows]
        xdec = (xc_c * e_out).astype(BF16)
        xc_bf = xc_c.astype(BF16)
        y_parts = []
        for g in range(SSD_GROUPS):
            bg = xbc[rows, SSD_WIDTH + g * SSD_STATE:SSD_WIDTH + (g + 1) * SSD_STATE].astype(BF16)
            cg = xbc[rows, SSD_WIDTH + (SSD_GROUPS + g) * SSD_STATE:
                     SSD_WIDTH + (SSD_GROUPS + g + 1) * SSD_STATE].astype(BF16)
            scores = lax.dot_general(cg, bg, (((1,), (1,)), ((), ())), preferred_element_type=F32)
            gs = slice(g * gw, (g + 1) * gw)
            st = state_ref[:, gs]
            y_off = jnp.dot(cg, st.astype(BF16), preferred_element_type=F32) * e_in[:, gs]
            diag = []
            for hh in range(hpg):
                h = g * hpg + hh
                lm = jnp.where(tri, jnp.exp(acs[:, h:h + 1] - acs_t[h:h + 1, :]), 0.0)
                p = (scores * lm).astype(BF16)
                diag.append(jnp.dot(p, xc_bf[:, h * SSD_HEAD_DIM:(h + 1) * SSD_HEAD_DIM],
                                    preferred_element_type=F32))
            y_parts.append(jnp.concatenate(diag, axis=1) + y_off)
            new = lax.dot_general(bg, xdec[:, gs], (((0,), (0,)), ((), ())),
                                  preferred_element_type=F32)
            state_ref[:, gs] = st * e_all[:, gs] + new
        y_chunks.append(jnp.concatenate(y_parts, axis=1))
    y = jnp.concatenate(y_chunks, axis=0) + dexp_ref[...] * xs
    yg = y * jax.nn.silu(z_ref[...])
    y_ref[:, LRU_WIDTH:] = _rms(yg, nw_ref[...]).astype(y_ref.dtype)


def _mixer(proj, bsz, seq, lcw, lcb, wg, ba, bx, lam, scw, scb, dtb, alog, dexp, nw, expand):
    nj = seq // SEQ_TILE

    def col(width, idx):
        return pl.BlockSpec((SEQ_TILE, width), lambda b, j: (b * nj + j, idx))

    def full(arr):
        nd = arr.ndim
        return pl.BlockSpec(arr.shape, lambda b, j: (0,) * nd)

    consts = (lcw, lcb, wg, ba, bx, lam, scw, scb, dtb, alog, dexp, nw, expand)
    return pl.pallas_call(
        _mixer_kernel,
        grid=(bsz, nj),
        in_specs=[col(1024, 0), col(1024, 1), col(1024, 2), col(SSD_CONV_DIM, 2),
                  col(DT_PAD, (3 * 1024 + SSD_CONV_DIM) // DT_PAD)] + [full(c) for c in consts],
        out_specs=pl.BlockSpec((SEQ_TILE, 2 * 1024), lambda b, j: (b * nj + j, 0)),
        out_shape=jax.ShapeDtypeStruct((bsz * seq, 2 * 1024), BF16),
        scratch_shapes=[
            pltpu.VMEM((SUBLANES, LRU_WIDTH), F32),
            pltpu.VMEM((SUBLANES, SSD_CONV_DIM), F32),
            pltpu.VMEM((SUBLANES, LRU_WIDTH), F32),
            pltpu.VMEM((SSD_STATE, SSD_WIDTH), F32),
        ],
        compiler_params=pltpu.CompilerParams(
            dimension_semantics=("arbitrary", "arbitrary"), vmem_limit_bytes=VMEM_LIMIT),
        name="mixer",
    )(proj, proj, proj, proj, proj, *consts)


def _sort_desc(v):
    v = list(v)
    n = len(v)
    k = 2
    while k <= n:
        j = k // 2
        while j >= 1:
            for i in range(n):
                l = i ^ j
                if l > i:
                    hi, lo = jnp.maximum(v[i], v[l]), jnp.minimum(v[i], v[l])
                    v[i], v[l] = (hi, lo) if (i & k) == 0 else (lo, hi)
            j //= 2
        k *= 2
    return v


def _merge_top(a, b):
    n = len(a)
    v = [jnp.maximum(a[i], b[n - 1 - i]) for i in range(n)]
    j = n // 2
    while j >= 1:
        for i in range(n):
            l = i ^ j
            if l > i:
                v[i], v[l] = jnp.maximum(v[i], v[l]), jnp.minimum(v[i], v[l])
        j //= 2
    return v


def _top_sorted(vals):
    groups = [_sort_desc(vals[i:i + PEER_TOPK]) for i in range(0, len(vals), PEER_TOPK)]
    while len(groups) > 1:
        nxt = [_merge_top(groups[i], groups[i + 1]) for i in range(0, len(groups) - 1, 2)]
        if len(groups) % 2:
            nxt.append(groups[-1])
        groups = nxt
    return groups[0]


def _scores_kernel(y_ref, x_ref, wout_ref, nw_ref, wq_ref, k1_ref, k2_ref,
                   x1_ref, hnt_ref, s1_ref, s2_ref):
    x1 = x_ref[...] + jnp.dot(y_ref[...], wout_ref[...], preferred_element_type=F32)
    x1_ref[...] = x1
    hn = _rms(x1, nw_ref[...]).astype(BF16)
    hnt_ref[...] = hn.T
    q = jnp.dot(hn, wq_ref[...], preferred_element_type=F32).astype(BF16)
    half = PEER_HEADS * PEER_HALF
    nt = (((1,), (1,)), ((), ()))
    s1_ref[...] = lax.dot_general(k1_ref[...], q[:, :half], nt, preferred_element_type=F32)
    s2_ref[...] = lax.dot_general(k2_ref[...], q[:, half:], nt, preferred_element_type=F32)


def _scores(ycat, x2, w_out, nw, wq, k1, k2):
    t = x2.shape[0]
    nrow = PEER_HEADS * PEER_N_KEYS
    tok = pl.BlockSpec((SCORE_TILE, D_MODEL), lambda i: (i, 0))
    tab = pl.BlockSpec((nrow, SCORE_TILE), lambda i: (0, i))

    def full(arr):
        nd = arr.ndim
        return pl.BlockSpec(arr.shape, lambda i: (0,) * nd, pipeline_mode=pl.Buffered(1))

    return pl.pallas_call(
        _scores_kernel,
        grid=(t // SCORE_TILE,),
        in_specs=[tok, tok, full(w_out), full(nw), full(wq), full(k1), full(k2)],
        out_specs=[tok, pl.BlockSpec((D_MODEL, SCORE_TILE), lambda i: (0, i)), tab, tab],
        out_shape=[
            jax.ShapeDtypeStruct((t, D_MODEL), F32),
            jax.ShapeDtypeStruct((D_MODEL, t), BF16),
            jax.ShapeDtypeStruct((nrow, t), F32),
            jax.ShapeDtypeStruct((nrow, t), F32),
        ],
        compiler_params=pltpu.CompilerParams(
            dimension_semantics=("arbitrary",), vmem_limit_bytes=VMEM_LIMIT),
        name="scores",
    )(ycat, x2, w_out, nw, wq, k1, k2)


def _router_kernel(s1_ref, s2_ref, cnt_ref, a1_ref, rank_ref, a2_ref):
    tq = s1_ref.shape[1]
    s1 = s1_ref[...].reshape(PEER_N_KEYS, PEER_HEADS, tq)
    s2 = s2_ref[...]

    a = _top_sorted([s1[n] for n in range(PEER_N_KEYS)])
    b = _top_sorted([s2_ref[pl.ds(n, PEER_HEADS, stride=PEER_N_KEYS), :] for n in range(PEER_N_KEYS)])
    cands = [a[i] + b[j] for i in range(PEER_TOPK) for j in range(PEER_TOPK)
             if (i + 1) * (j + 1) <= PEER_TOPK]
    neg = jnp.full_like(a[0], -jnp.inf)
    tau = _top_sorted(cands + [neg] * (-len(cands) % PEER_TOPK))[PEER_TOPK - 1]
    m = a[0] + b[0]
    z = jnp.zeros_like(m)
    for c in cands:
        z = z + jnp.where(c >= tau, jnp.exp(c - m), 0.0)
    inv_z = 1.0 / z

    cnt = jnp.zeros_like(s1)
    for j in range(PEER_TOPK):
        cnt = cnt + jnp.where(s1 + b[j] >= tau, 1.0, 0.0)
    cnt_ref[...] = cnt.reshape(PEER_N_KEYS * PEER_HEADS, tq)
    a1_ref[...] = jnp.exp(s1 - a[0]).reshape(PEER_N_KEYS * PEER_HEADS, tq)
    for h in range(PEER_HEADS):
        rows = slice(h * PEER_N_KEYS, (h + 1) * PEER_N_KEYS)
        s2h = s2[rows, :]
        rank = jnp.zeros_like(s2h)
        for k in range(PEER_TOPK):
            rank = rank + jnp.where(b[k][h:h + 1, :] > s2h, 1.0, 0.0)
        rank_ref[rows, :] = rank.astype(rank_ref.dtype)
        a2_ref[rows, :] = (jnp.exp(s2h - b[0][h:h + 1, :]) * inv_z[h:h + 1, :]).astype(a2_ref.dtype)


def _router(s1, s2):
    nrow, t = s1.shape
    tab = pl.BlockSpec((nrow, ROUTER_TILE), lambda i: (0, i))
    return pl.pallas_call(
        _router_kernel,
        grid=(t // ROUTER_TILE,),
        in_specs=[tab, tab],
        out_specs=[tab, tab, tab, tab],
        out_shape=[jax.ShapeDtypeStruct((nrow, t), F32)] * 4,
        compiler_params=pltpu.CompilerParams(
            dimension_semantics=("arbitrary",), vmem_limit_bytes=VMEM_LIMIT),
        name="router",
    )(s1, s2)


def _peer_kernel(hnt_ref, u_ref, vt_ref, rank32_ref, a232_ref, cnt_ref, a1_ref, x1_ref, nw_ref,
                 o_ref, acc_ref, act0_ref, act1_ref, cntb_ref, a1b_ref, rank_ref, a2_ref):
    s = pl.program_id(1)
    n_steps = pl.num_programs(1)

    @pl.when((pl.program_id(0) == 0) & (s == 0))
    def _():
        act1_ref[...] = jnp.zeros_like(act1_ref)
        acc_ref[...] = jnp.zeros_like(acc_ref)

    @pl.when(s == PEER_SKEW)
    def _():
        acc_ref[...] = jnp.zeros_like(acc_ref)

    @pl.when(s == 0)
    def _():
        rank_ref[...] = rank32_ref[...].astype(BF16)
        a2_ref[...] = a232_ref[...].astype(BF16)

    pack = 2 * SUBLANES

    def bf16_rows(ref, r, lanes):
        return jnp.tile(ref[r * pack:(r + 1) * pack, lanes], (PEER_N_KEYS // pack, 1))

    def gated(act_r, j, c):
        lanes = slice(c * LANES, (c + 1) * LANES)
        w = None
        for h in range(PEER_HEADS):
            r = j * PEER_HEADS + h
            rows = slice(h * PEER_N_KEYS, (h + 1) * PEER_N_KEYS)
            term = jnp.where(rank_ref[rows, lanes] < bf16_rows(cntb_ref, r, lanes),
                             a2_ref[rows, lanes] * bf16_rows(a1b_ref, r, lanes),
                             jnp.zeros((), BF16))
            w = term if w is None else w + term
        return jax.nn.gelu(act_r[j * PEER_N_KEYS:(j + 1) * PEER_N_KEYS, lanes]).astype(BF16) * w

    def stages(act_r, act_w):
        for src, dst in ((cnt_ref, cntb_ref), (a1_ref, a1b_ref)):
            for r in range(src.shape[0]):
                dst[r * pack:(r + 1) * pack, :] = jnp.broadcast_to(
                    src[r:r + 1, :], (pack, PEER_TQ)).astype(BF16)
        per = MXU_WIDTH // LANES
        for p in range(PEER_TQ // MXU_WIDTH):
            lanes = slice(p * MXU_WIDTH, (p + 1) * MXU_WIDTH)
            wact = jnp.concatenate(
                [jnp.concatenate([gated(act_r, j, p * per + cc) for cc in range(per)], axis=1)
                 for j in range(PEER_TE // PEER_N_KEYS)], axis=0)
            act_w[:, lanes] = jnp.dot(u_ref[...], hnt_ref[:, lanes], preferred_element_type=F32)
            acc_ref[:, lanes] += jnp.dot(vt_ref[...], wact, preferred_element_type=F32)

    @pl.when(s % 2 == 0)
    def _():
        stages(act1_ref, act0_ref)

    @pl.when(s % 2 == 1)
    def _():
        stages(act0_ref, act1_ref)

    @pl.when(s == n_steps - 1)
    def _():
        o_ref[...] = _rms(x1_ref[...] + acc_ref[...].T, nw_ref[...])


def _peer(hnt, u_bf, vt_bf, rank, a2, cnt, a1, x1, nw):
    t = x1.shape[0]
    n_exp = u_bf.shape[0]
    nrow = PEER_HEADS * PEER_N_KEYS
    rows_per_step = PEER_HEADS * (PEER_TE // PEER_N_KEYS)
    n_tiles = n_exp // PEER_TE
    tile = lambda s, lag: jnp.clip(s - lag, 0, n_tiles - 1)
    tok_tab = pl.BlockSpec((nrow, PEER_TQ), lambda i, s: (0, i))
    exp_tab = pl.BlockSpec((rows_per_step, PEER_TQ), lambda i, s: (tile(s, PEER_SKEW), i))
    tok = pl.BlockSpec((PEER_TQ, D_MODEL), lambda i, s: (i, 0))
    return pl.pallas_call(
        _peer_kernel,
        grid=(t // PEER_TQ, n_tiles + PEER_SKEW),
        in_specs=[
            pl.BlockSpec((D_MODEL, PEER_TQ), lambda i, s: (0, i)),
            pl.BlockSpec((PEER_TE, D_MODEL), lambda i, s: (tile(s, 0), 0)),
            pl.BlockSpec((None, D_MODEL, PEER_TE), lambda i, s: (tile(s, PEER_SKEW), 0, 0)),
            tok_tab, tok_tab, exp_tab, exp_tab, tok,
            pl.BlockSpec((1, D_MODEL), lambda i, s: (0, 0)),
        ],
        out_specs=tok,
        out_shape=jax.ShapeDtypeStruct((t, D_MODEL), F32),
        scratch_shapes=[pltpu.VMEM((D_MODEL, PEER_TQ), F32),
                        pltpu.VMEM((PEER_TE, PEER_TQ), F32),
                        pltpu.VMEM((PEER_TE, PEER_TQ), F32),
                        pltpu.VMEM((rows_per_step * 2 * SUBLANES, PEER_TQ), BF16),
                        pltpu.VMEM((rows_per_step * 2 * SUBLANES, PEER_TQ), BF16),
                        pltpu.VMEM((nrow, PEER_TQ), BF16),
                        pltpu.VMEM((nrow, PEER_TQ), BF16)],
        compiler_params=pltpu.CompilerParams(
            dimension_semantics=("arbitrary", "arbitrary"), vmem_limit_bytes=VMEM_LIMIT),
        name="peer",
    )(hnt, u_bf, vt_bf, rank, a2, cnt, a1, x1, nw)


def _block_diag_chunks(w):
    per = GATE_CHUNK // LRU_HEAD_DIM
    w4 = w.reshape(LRU_HEADS // per, per, LRU_HEAD_DIM, LRU_HEAD_DIM)
    bd = jnp.einsum('cgij,gk->cgikj', w4, jnp.eye(per, dtype=w.dtype))
    return bd.reshape(LRU_HEADS // per, GATE_CHUNK, GATE_CHUNK)


def kernel(x, norm_mix_w, w_in, lru_conv_w, lru_conv_b, lru_wa, lru_ba, lru_wx, lru_bx, lru_lambda, ssd_conv_w, ssd_conv_b, ssd_dt_bias, ssd_a_log, ssd_d, ssd_norm_w, w_out, norm_ffn_w, peer_wq, peer_sub_keys, peer_u, peer_v, norm_final_w):
    bsz, seq, _ = x.shape
    assert norm_mix_w.shape[0] == 1, "one layer"
    t = bsz * seq
    assert seq % SEQ_TILE == 0 and t % PEER_TQ == 0 and t % IN_TILE == 0 and t % SCORE_TILE == 0
    x2 = x.reshape(t, D_MODEL)
    row = lambda v: v.reshape(1, -1).astype(F32)

    w_in_pad = jnp.pad(w_in[0], ((0, 0), (0, PROJ_PAD - w_in.shape[-1]))).astype(BF16)
    wg = jnp.concatenate([_block_diag_chunks(lru_wa[0]), _block_diag_chunks(lru_wx[0])], axis=-1).astype(BF16)
    pad_heads = lambda v: jnp.pad(v.reshape(1, -1).astype(F32), ((0, 0), (0, DT_PAD - SSD_HEADS)))
    expand = (jnp.arange(DT_PAD)[:, None] == (jnp.arange(SSD_WIDTH) // SSD_HEAD_DIM)[None, :]).astype(F32)
    dexp = jnp.repeat(ssd_d[0].astype(F32), SSD_HEAD_DIM).reshape(1, -1)
    wq = peer_wq[0].reshape(D_MODEL, PEER_HEADS, 2, PEER_HALF).transpose(0, 2, 1, 3)
    wq = wq.reshape(D_MODEL, 2 * PEER_HEADS * PEER_HALF).astype(BF16)
    keys = peer_sub_keys[0]
    eye = jnp.eye(PEER_HEADS, dtype=keys.dtype)
    k1 = jnp.einsum('hnd,hg->nhgd', keys[:, 0], eye).reshape(PEER_N_KEYS * PEER_HEADS, -1).astype(BF16)
    k2 = jnp.einsum('hnd,hg->hngd', keys[:, 1], eye).reshape(PEER_HEADS * PEER_N_KEYS, -1).astype(BF16)
    u_bf = peer_u[0].astype(BF16)
    vt_bf = peer_v[0].reshape(-1, PEER_TE, D_MODEL).transpose(0, 2, 1).astype(BF16)

    proj = _in_proj(x2, row(norm_mix_w[0]), w_in_pad)
    ycat = _mixer(proj, bsz, seq,
                  lru_conv_w[0].astype(F32), row(lru_conv_b[0]), wg, row(lru_ba[0]), row(lru_bx[0]),
                  row(lru_lambda[0]), ssd_conv_w[0].astype(F32), row(ssd_conv_b[0]),
                  pad_heads(ssd_dt_bias[0]), pad_heads(ssd_a_log[0]), dexp, row(ssd_norm_w[0]), expand)
    x1, hnt, s1, s2 = _scores(ycat, x2, w_out[0].astype(BF16), row(norm_ffn_w[0]), wq, k1, k2)
    cnt, a1, rank, a2 = _router(s1, s2)
    out = _peer(hnt, u_bf, vt_bf, rank, a2, cnt, a1, x1, row(norm_final_w))
    return out.reshape(bsz, seq, D_MODEL)
```

```python
import functools
import math

import jax
import jax.numpy as jnp
from jax import lax
from jax.experimental import pallas as pl
from jax.experimental.pallas import tpu as pltpu

F32 = jnp.float32
BF16 = jnp.bfloat16

D_MODEL = 2048
CONV_WIDTH = 4
LRU_WIDTH = 1024
LRU_HEADS = 16
LRU_HEAD_DIM = 64
LRU_C = 8.0
SSD_WIDTH = 1024
SSD_HEAD_DIM = 64
SSD_HEADS = 16
SSD_GROUPS = 2
SSD_STATE = 128
SSD_CONV_DIM = SSD_WIDTH + 2 * SSD_GROUPS * SSD_STATE
PEER_HEADS = 8
PEER_N_KEYS = 128
PEER_HALF = 128
PEER_TOPK = 16
EPS = 1e-6

LANES = 128
SUBLANES = 8
MXU_WIDTH = 256
DT_PAD = LANES
PROJ_PAD = 3 * 1024 + SSD_CONV_DIM + DT_PAD
GATE_CHUNK = 256

IN_TILE = 256
SEQ_TILE = 256
SSD_CHUNK = 128
SCORE_TILE = 256
ROUTER_TILE = 128
PEER_TQ = 512
PEER_TE = 1024
GATE_KEYS = 4
PEER_SKEW = 1
VMEM_LIMIT = 56 * 1024 * 1024


def _rms(x, w):
    return x * lax.rsqrt(jnp.mean(x * x, axis=-1, keepdims=True) + EPS) * w


def _in_proj_kernel(x_ref, nw_ref, w_ref, o_ref):
    hn = _rms(x_ref[...], nw_ref[...])
    o_ref[...] = jnp.dot(hn.astype(BF16), w_ref[...], preferred_element_type=F32)


def _in_proj(x2, norm_w, w_in_pad):
    t = x2.shape[0]
    return pl.pallas_call(
        _in_proj_kernel,
        grid=(t // IN_TILE,),
        in_specs=[
            pl.BlockSpec((IN_TILE, D_MODEL), lambda i: (i, 0)),
            pl.BlockSpec((1, D_MODEL), lambda i: (0, 0)),
            pl.BlockSpec((D_MODEL, PROJ_PAD), lambda i: (0, 0), pipeline_mode=pl.Buffered(1)),
        ],
        out_specs=pl.BlockSpec((IN_TILE, PROJ_PAD), lambda i: (i, 0)),
        out_shape=jax.ShapeDtypeStruct((t, PROJ_PAD), F32),
        compiler_params=pltpu.CompilerParams(
            dimension_semantics=("arbitrary",), vmem_limit_bytes=VMEM_LIMIT),
        name="in_proj",
    )(x2, norm_w, w_in_pad)


def _shift_rows(x, d, fill):
    rolled = pltpu.roll(x, d, axis=0)
    row = lax.broadcasted_iota(jnp.int32, x.shape, 0)
    return jnp.where(row < d, fill, rolled)


def _causal_conv(cur, prev8, w_ref, b_ref):
    row8 = lax.broadcasted_iota(jnp.int32, prev8.shape, 0)
    out = cur * w_ref[CONV_WIDTH - 1:CONV_WIDTH, :] + b_ref[...]
    for d in range(1, CONV_WIDTH):
        rolled = pltpu.roll(cur, d, axis=0)
        top = jnp.where(row8 < d, pltpu.roll(prev8, d, axis=0), rolled[:SUBLANES])
        shifted = jnp.concatenate([top, rolled[SUBLANES:]], axis=0)
        k = CONV_WIDTH - 1 - d
        out = out + shifted * w_ref[k:k + 1, :]
    return out


def _mixer_kernel(lx_ref, lg_ref, z_ref, xbc_ref, dt_ref,
                  lcw_ref, lcb_ref, wg_ref, ba_ref, bx_ref, lam_ref,
                  scw_ref, scb_ref, dtb_ref, alog_ref, dexp_ref, nw_ref, expand_ref,
                  y_ref,
                  lprev_ref, sprev_ref, h_ref, state_ref):
    ts = lx_ref.shape[0]

    @pl.when(pl.program_id(1) == 0)
    def _():
        lprev_ref[...] = jnp.zeros_like(lprev_ref)
        sprev_ref[...] = jnp.zeros_like(sprev_ref)
        h_ref[...] = jnp.zeros_like(h_ref)
        state_ref[...] = jnp.zeros_like(state_ref)

    lx = lx_ref[...]
    xl = _causal_conv(lx, lprev_ref[...], lcw_ref, lcb_ref)
    lprev_ref[...] = lx[ts - SUBLANES:, :]
    xl_bf = xl.astype(BF16)
    ra, ri = [], []
    for c in range(LRU_WIDTH // GATE_CHUNK):
        g = jnp.dot(xl_bf[:, c * GATE_CHUNK:(c + 1) * GATE_CHUNK], wg_ref[c],
                    preferred_element_type=F32)
        ra.append(g[:, :GATE_CHUNK])
        ri.append(g[:, GATE_CHUNK:])
    r = jax.nn.sigmoid(jnp.concatenate(ra, axis=1) + ba_ref[...])
    i = jax.nn.sigmoid(jnp.concatenate(ri, axis=1) + bx_ref[...])
    log_a = -LRU_C * r * jax.nn.softplus(-lam_ref[...])
    a = jnp.exp(log_a)
    b2 = jnp.tanh(-log_a) * (1.0 + a * a)
    b = jnp.where(b2 > 0.0, b2 * lax.rsqrt(b2), 0.0) * (i * xl)
    in_group = lax.broadcasted_iota(jnp.int32, a.shape, 0) % SUBLANES
    d = 1
    while d < SUBLANES:
        b = a * jnp.where(in_group < d, 0.0, pltpu.roll(b, d, axis=0)) + b
        a = a * jnp.where(in_group < d, 1.0, pltpu.roll(a, d, axis=0))
        d *= 2
    h = h_ref[0:1, :]
    groups = []
    for g in range(ts // SUBLANES):
        rows = slice(g * SUBLANES, (g + 1) * SUBLANES)
        hg = a[rows] * h + b[rows]
        groups.append(hg)
        h = hg[SUBLANES - 1:SUBLANES, :]
    hs = jnp.concatenate(groups, axis=0)
    h_ref[...] = jnp.broadcast_to(h, h_ref.shape)
    y_ref[:, :LRU_WIDTH] = (hs * jax.nn.gelu(lg_ref[...])).astype(y_ref.dtype)

    xbc_raw = xbc_ref[...]
    xbc = jax.nn.silu(_causal_conv(xbc_raw, sprev_ref[...], scw_ref, scb_ref))
    sprev_ref[...] = xbc_raw[ts - SUBLANES:, :]
    xs = xbc[:, :SSD_WIDTH]
    dt = jax.nn.softplus(dt_ref[...] + dtb_ref[...])
    adt = -jnp.exp(alog_ref[...]) * dt
    expand = expand_ref[...]

    def widen(v):
        return jnp.dot(v, expand, precision=lax.Precision.HIGHEST, preferred_element_type=F32)

    xc = xs * widen(dt)
    tri = (lax.broadcasted_iota(jnp.int32, (SSD_CHUNK, SSD_CHUNK), 0)
           >= lax.broadcasted_iota(jnp.int32, (SSD_CHUNK, SSD_CHUNK), 1))
    hpg = SSD_HEADS // SSD_GROUPS
    gw = hpg * SSD_HEAD_DIM
    y_chunks = []
    for c in range(ts // SSD_CHUNK):
        rows = slice(c * SSD_CHUNK, (c + 1) * SSD_CHUNK)
        acs = adt[rows]
        d = 1
        while d < SSD_CHUNK:
            acs = acs + _shift_rows(acs, d, 0.0)
            d *= 2
        acs_t = acs.T
        total = acs[SSD_CHUNK - 1:SSD_CHUNK, :]
        e_in = widen(jnp.exp(acs))
        e_out = widen(jnp.exp(total - acs))
        e_all = widen(jnp.exp(total) * jnp.ones((SUBLANES, 1), F32))[0:1, :]
        xc_c = xc[rows]
        xdec = (xc_c * e_out).astype(BF16)
        xc_bf = xc_c.astype(BF16)
        y_parts = []
        for g in range(SSD_GROUPS):
            bg = xbc[rows, SSD_WIDTH + g * SSD_STATE:SSD_WIDTH + (g + 1) * SSD_STATE].astype(BF16)
            cg = xbc[rows, SSD_WIDTH + (SSD_GROUPS + g) * SSD_STATE:
                     SSD_WIDTH + (SSD_GROUPS + g + 1) * SSD_STATE].astype(BF16)
            scores = lax.dot_general(cg, bg, (((1,), (1,)), ((), ())), preferred_element_type=F32)
            gs = slice(g * gw, (g + 1) * gw)
            st = state_ref[:, gs]
            y_off = jnp.dot(cg, st.astype(BF16), preferred_element_type=F32) * e_in[:, gs]
            diag = []
            for hh in range(hpg):
                h = g * hpg + hh
                lm = jnp.where(tri, jnp.exp(acs[:, h:h + 1] - acs_t[h:h + 1, :]), 0.0)
                p = (scores * lm).astype(BF16)
                diag.append(jnp.dot(p, xc_bf[:, h * SSD_HEAD_DIM:(h + 1) * SSD_HEAD_DIM],
                                    preferred_element_type=F32))
            y_parts.append(jnp.concatenate(diag, axis=1) + y_off)
            new = lax.dot_general(bg, xdec[:, gs], (((0,), (0,)), ((), ())),
                                  preferred_element_type=F32)
            state_ref[:, gs] = st * e_all[:, gs] + new
        y_chunks.append(jnp.concatenate(y_parts, axis=1))
    y = jnp.concatenate(y_chunks, axis=0) + dexp_ref[...] * xs
    yg = y * jax.nn.silu(z_ref[...])
    y_ref[:, LRU_WIDTH:] = _rms(yg, nw_ref[...]).astype(y_ref.dtype)


def _mixer(proj, bsz, seq, lcw, lcb, wg, ba, bx, lam, scw, scb, dtb, alog, dexp, nw, expand):
    nj = seq // SEQ_TILE

    def col(width, idx):
        return pl.BlockSpec((SEQ_TILE, width), lambda b, j: (b * nj + j, idx))

    def full(arr):
        nd = arr.ndim
        return pl.BlockSpec(arr.shape, lambda b, j: (0,) * nd)

    consts = (lcw, lcb, wg, ba, bx, lam, scw, scb, dtb, alog, dexp, nw, expand)
    return pl.pallas_call(
        _mixer_kernel,
        grid=(bsz, nj),
        in_specs=[col(1024, 0), col(1024, 1), col(1024, 2), col(SSD_CONV_DIM, 2),
                  col(DT_PAD, (3 * 1024 + SSD_CONV_DIM) // DT_PAD)] + [full(c) for c in consts],
        out_specs=pl.BlockSpec((SEQ_TILE, 2 * 1024), lambda b, j: (b * nj + j, 0)),
        out_shape=jax.ShapeDtypeStruct((bsz * seq, 2 * 1024), BF16),
        scratch_shapes=[
            pltpu.VMEM((SUBLANES, LRU_WIDTH), F32),
            pltpu.VMEM((SUBLANES, SSD_CONV_DIM), F32),
            pltpu.VMEM((SUBLANES, LRU_WIDTH), F32),
            pltpu.VMEM((SSD_STATE, SSD_WIDTH), F32),
        ],
        compiler_params=pltpu.CompilerParams(
            dimension_semantics=("arbitrary", "arbitrary"), vmem_limit_bytes=VMEM_LIMIT),
        name="mixer",
    )(proj, proj, proj, proj, proj, *consts)


def _sort_desc(v):
    v = list(v)
    n = len(v)
    k = 2
    while k <= n:
        j = k // 2
        while j >= 1:
            for i in range(n):
                l = i ^ j
                if l > i:
                    hi, lo = jnp.maximum(v[i], v[l]), jnp.minimum(v[i], v[l])
                    v[i], v[l] = (hi, lo) if (i & k) == 0 else (lo, hi)
            j //= 2
        k *= 2
    return v


def _merge_top(a, b):
    n = len(a)
    v = [jnp.maximum(a[i], b[n - 1 - i]) for i in range(n)]
    j = n // 2
    while j >= 1:
        for i in range(n):
            l = i ^ j
            if l > i:
                v[i], v[l] = jnp.maximum(v[i], v[l]), jnp.minimum(v[i], v[l])
        j //= 2
    return v


def _top_sorted(vals):
    groups = [_sort_desc(vals[i:i + PEER_TOPK]) for i in range(0, len(vals), PEER_TOPK)]
    while len(groups) > 1:
        nxt = [_merge_top(groups[i], groups[i + 1]) for i in range(0, len(groups) - 1, 2)]
        if len(groups) % 2:
            nxt.append(groups[-1])
        groups = nxt
    return groups[0]


def _scores_kernel(y_ref, x_ref, wout_ref, nw_ref, wq_ref, k1_ref, k2_ref,
                   x1_ref, hnt_ref, s1_ref, s2_ref):
    x1 = x_ref[...] + jnp.dot(y_ref[...], wout_ref[...], preferred_element_type=F32)
    x1_ref[...] = x1
    hn = _rms(x1, nw_ref[...]).astype(BF16)
    hnt_ref[...] = hn.T
    q = jnp.dot(hn, wq_ref[...], preferred_element_type=F32).astype(BF16)
    half = PEER_HEADS * PEER_HALF
    nt = (((1,), (1,)), ((), ()))
    s1_ref[...] = lax.dot_general(k1_ref[...], q[:, :half], nt, preferred_element_type=F32)
    s2_ref[...] = lax.dot_general(k2_ref[...], q[:, half:], nt, preferred_element_type=F32)


def _scores(ycat, x2, w_out, nw, wq, k1, k2):
    t = x2.shape[0]
    nrow = PEER_HEADS * PEER_N_KEYS
    tok = pl.BlockSpec((SCORE_TILE, D_MODEL), lambda i: (i, 0))
    tab = pl.BlockSpec((nrow, SCORE_TILE), lambda i: (0, i))

    def full(arr):
        nd = arr.ndim
        return pl.BlockSpec(arr.shape, lambda i: (0,) * nd, pipeline_mode=pl.Buffered(1))

    return pl.pallas_call(
        _scores_kernel,
        grid=(t // SCORE_TILE,),
        in_specs=[tok, tok, full(w_out), full(nw), full(wq), full(k1), full(k2)],
        out_specs=[tok, pl.BlockSpec((D_MODEL, SCORE_TILE), lambda i: (0, i)), tab, tab],
        out_shape=[
            jax.ShapeDtypeStruct((t, D_MODEL), F32),
            jax.ShapeDtypeStruct((D_MODEL, t), BF16),
            jax.ShapeDtypeStruct((nrow, t), F32),
            jax.ShapeDtypeStruct((nrow, t), F32),
        ],
        compiler_params=pltpu.CompilerParams(
            dimension_semantics=("arbitrary",), vmem_limit_bytes=VMEM_LIMIT),
        name="scores",
    )(ycat, x2, w_out, nw, wq, k1, k2)


def _router_kernel(s1_ref, s2_ref, cnt_ref, a1_ref, rank_ref, a2_ref):
    tq = s1_ref.shape[1]
    s1 = s1_ref[...].reshape(PEER_N_KEYS, PEER_HEADS, tq)
    s2 = s2_ref[...]

    a = _top_sorted([s1[n] for n in range(PEER_N_KEYS)])
    b = _top_sorted([s2_ref[pl.ds(n, PEER_HEADS, stride=PEER_N_KEYS), :] for n in range(PEER_N_KEYS)])
    cands = [a[i] + b[j] for i in range(PEER_TOPK) for j in range(PEER_TOPK)
             if (i + 1) * (j + 1) <= PEER_TOPK]
    neg = jnp.full_like(a[0], -jnp.inf)
    tau = _top_sorted(cands + [neg] * (-len(cands) % PEER_TOPK))[PEER_TOPK - 1]
    m = a[0] + b[0]
    z = jnp.zeros_like(m)
    for c in cands:
        z = z + jnp.where(c >= tau, jnp.exp(c - m), 0.0)
    inv_z = 1.0 / z

    cnt = jnp.zeros_like(s1)
    for j in range(PEER_TOPK):
        cnt = cnt + jnp.where(s1 + b[j] >= tau, 1.0, 0.0)
    cnt_ref[...] = cnt.reshape(PEER_N_KEYS * PEER_HEADS, tq)
    a1_ref[...] = jnp.exp(s1 - a[0]).reshape(PEER_N_KEYS * PEER_HEADS, tq)
    for h in range(PEER_HEADS):
        rows = slice(h * PEER_N_KEYS, (h + 1) * PEER_N_KEYS)
        s2h = s2[rows, :]
        rank = jnp.zeros_like(s2h)
        for k in range(PEER_TOPK):
            rank = rank + jnp.where(b[k][h:h + 1, :] > s2h, 1.0, 0.0)
        rank_ref[rows, :] = rank.astype(rank_ref.dtype)
        a2_ref[rows, :] = (jnp.exp(s2h - b[0][h:h + 1, :]) * inv_z[h:h + 1, :]).astype(a2_ref.dtype)


def _router(s1, s2):
    nrow, t = s1.shape
    tab = pl.BlockSpec((nrow, ROUTER_TILE), lambda i: (0, i))
    return pl.pallas_call(
        _router_kernel,
        grid=(t // ROUTER_TILE,),
        in_specs=[tab, tab],
        out_specs=[tab, tab, tab, tab],
        out_shape=[jax.ShapeDtypeStruct((nrow, t), F32)] * 4,
        compiler_params=pltpu.CompilerParams(
            dimension_semantics=("arbitrary",), vmem_limit_bytes=VMEM_LIMIT),
        name="router",
    )(s1, s2)


def _peer_kernel(hnt_ref, u_ref, vt_ref, rank32_ref, a232_ref, cnt_ref, a1_ref, x1_ref, nw_ref,
                 o_ref, acc_ref, act0_ref, act1_ref, cntb_ref, a1b_ref, rank_ref, a2_ref):
    s = pl.program_id(1)
    n_steps = pl.num_programs(1)

    @pl.when((pl.program_id(0) == 0) & (s == 0))
    def _():
        act1_ref[...] = jnp.zeros_like(act1_ref)
        acc_ref[...] = jnp.zeros_like(acc_ref)

    @pl.when(s == PEER_SKEW)
    def _():
        acc_ref[...] = jnp.zeros_like(acc_ref)

    @pl.when(s == 0)
    def _():
        rank_ref[...] = rank32_ref[...].astype(BF16)
        a2_ref[...] = a232_ref[...].astype(BF16)

    pack = 2 * SUBLANES

    def bf16_rows(ref, r, lanes):
        return jnp.tile(ref[r * pack:(r + 1) * pack, lanes], (PEER_N_KEYS // pack, 1))

    def gated(act_r, js, c):
        lanes = slice(c * LANES, (c + 1) * LANES)
        w = [None] * len(js)
        for h in range(PEER_HEADS):
            rows = slice(h * PEER_N_KEYS, (h + 1) * PEER_N_KEYS)
            rank_h, a2_h = rank_ref[rows, lanes], a2_ref[rows, lanes]
            for k, j in enumerate(js):
                r = j * PEER_HEADS + h
                term = jnp.where(rank_h < bf16_rows(cntb_ref, r, lanes),
                                 a2_h * bf16_rows(a1b_ref, r, lanes), jnp.zeros((), BF16))
                w[k] = term if w[k] is None else w[k] + term
        return [jax.nn.gelu(act_r[j * PEER_N_KEYS:(j + 1) * PEER_N_KEYS, lanes]).astype(BF16) * w[k]
                for k, j in enumerate(js)]

    def stages(act_r, act_w):
        for src, dst in ((cnt_ref, cntb_ref), (a1_ref, a1b_ref)):
            for r in range(src.shape[0]):
                dst[r * pack:(r + 1) * pack, :] = jnp.broadcast_to(
                    src[r:r + 1, :], (pack, PEER_TQ)).astype(BF16)
        per = MXU_WIDTH // LANES
        n_keys = PEER_TE // PEER_N_KEYS
        for p in range(PEER_TQ // MXU_WIDTH):
            lanes = slice(p * MXU_WIDTH, (p + 1) * MXU_WIDTH)
            cols = []
            for cc in range(per):
                blocks = []
                for j0 in range(0, n_keys, GATE_KEYS):
                    blocks += gated(act_r, range(j0, j0 + GATE_KEYS), p * per + cc)
                cols.append(jnp.concatenate(blocks, axis=0))
            wact = jnp.concatenate(cols, axis=1)
            act_w[:, lanes] = jnp.dot(u_ref[...], hnt_ref[:, lanes], preferred_element_type=F32)
            acc_ref[:, lanes] += jnp.dot(vt_ref[...], wact, preferred_element_type=F32)

    @pl.when(s % 2 == 0)
    def _():
        stages(act1_ref, act0_ref)

    @pl.when(s % 2 == 1)
    def _():
        stages(act0_ref, act1_ref)

    @pl.when(s == n_steps - 1)
    def _():
        o_ref[...] = _rms(x1_ref[...] + acc_ref[...].T, nw_ref[...])


def _peer(hnt, u_bf, vt_bf, rank, a2, cnt, a1, x1, nw):
    t = x1.shape[0]
    n_exp = u_bf.shape[0]
    nrow = PEER_HEADS * PEER_N_KEYS
    rows_per_step = PEER_HEADS * (PEER_TE // PEER_N_KEYS)
    n_tiles = n_exp // PEER_TE
    tile = lambda s, lag: jnp.clip(s - lag, 0, n_tiles - 1)
    tok_tab = pl.BlockSpec((nrow, PEER_TQ), lambda i, s: (0, i))
    exp_tab = pl.BlockSpec((rows_per_step, PEER_TQ), lambda i, s: (tile(s, PEER_SKEW), i))
    tok = pl.BlockSpec((PEER_TQ, D_MODEL), lambda i, s: (i, 0))
    once = lambda spec: pl.BlockSpec(spec.block_shape, spec.index_map, pipeline_mode=pl.Buffered(1))
    return pl.pallas_call(
        _peer_kernel,
        grid=(t // PEER_TQ, n_tiles + PEER_SKEW),
        in_specs=[
            pl.BlockSpec((D_MODEL, PEER_TQ), lambda i, s: (0, i)),
            pl.BlockSpec((PEER_TE, D_MODEL), lambda i, s: (tile(s, 0), 0)),
            pl.BlockSpec((None, D_MODEL, PEER_TE), lambda i, s: (tile(s, PEER_SKEW), 0, 0)),
            once(tok_tab), once(tok_tab), exp_tab, exp_tab, once(tok),
            pl.BlockSpec((1, D_MODEL), lambda i, s: (0, 0)),
        ],
        out_specs=tok,
        out_shape=jax.ShapeDtypeStruct((t, D_MODEL), F32),
        scratch_shapes=[pltpu.VMEM((D_MODEL, PEER_TQ), F32),
                        pltpu.VMEM((PEER_TE, PEER_TQ), F32),
                        pltpu.VMEM((PEER_TE, PEER_TQ), F32),
                        pltpu.VMEM((rows_per_step * 2 * SUBLANES, PEER_TQ), BF16),
                        pltpu.VMEM((rows_per_step * 2 * SUBLANES, PEER_TQ), BF16),
                        pltpu.VMEM((nrow, PEER_TQ), BF16),
                        pltpu.VMEM((nrow, PEER_TQ), BF16)],
        compiler_params=pltpu.CompilerParams(
            dimension_semantics=("arbitrary", "arbitrary"), vmem_limit_bytes=VMEM_LIMIT),
        name="peer",
    )(hnt, u_bf, vt_bf, rank, a2, cnt, a1, x1, nw)


def _block_diag_chunks(w):
    per = GATE_CHUNK // LRU_HEAD_DIM
    w4 = w.reshape(LRU_HEADS // per, per, LRU_HEAD_DIM, LRU_HEAD_DIM)
    bd = jnp.einsum('cgij,gk->cgikj', w4, jnp.eye(per, dtype=w.dtype))
    return bd.reshape(LRU_HEADS // per, GATE_CHUNK, GATE_CHUNK)


def kernel(x, norm_mix_w, w_in, lru_conv_w, lru_conv_b, lru_wa, lru_ba, lru_wx, lru_bx, lru_lambda, ssd_conv_w, ssd_conv_b, ssd_dt_bias, ssd_a_log, ssd_d, ssd_norm_w, w_out, norm_ffn_w, peer_wq, peer_sub_keys, peer_u, peer_v, norm_final_w):
    bsz, seq, _ = x.shape
    assert norm_mix_w.shape[0] == 1, "one layer"
    t = bsz * seq
    assert seq % SEQ_TILE == 0 and t % PEER_TQ == 0 and t % IN_TILE == 0 and t % SCORE_TILE == 0
    x2 = x.reshape(t, D_MODEL)
    row = lambda v: v.reshape(1, -1).astype(F32)

    w_in_pad = jnp.pad(w_in[0], ((0, 0), (0, PROJ_PAD - w_in.shape[-1]))).astype(BF16)
    wg = jnp.concatenate([_block_diag_chunks(lru_wa[0]), _block_diag_chunks(lru_wx[0])], axis=-1).astype(BF16)
    pad_heads = lambda v: jnp.pad(v.reshape(1, -1).astype(F32), ((0, 0), (0, DT_PAD - SSD_HEADS)))
    expand = (jnp.arange(DT_PAD)[:, None] == (jnp.arange(SSD_WIDTH) // SSD_HEAD_DIM)[None, :]).astype(F32)
    dexp = jnp.repeat(ssd_d[0].astype(F32), SSD_HEAD_DIM).reshape(1, -1)
    wq = peer_wq[0].reshape(D_MODEL, PEER_HEADS, 2, PEER_HALF).transpose(0, 2, 1, 3)
    wq = wq.reshape(D_MODEL, 2 * PEER_HEADS * PEER_HALF).astype(BF16)
    keys = peer_sub_keys[0]
    eye = jnp.eye(PEER_HEADS, dtype=keys.dtype)
    k1 = jnp.einsum('hnd,hg->nhgd', keys[:, 0], eye).reshape(PEER_N_KEYS * PEER_HEADS, -1).astype(BF16)
    k2 = jnp.einsum('hnd,hg->hngd', keys[:, 1], eye).reshape(PEER_HEADS * PEER_N_KEYS, -1).astype(BF16)
    u_bf = peer_u[0].astype(BF16)
    vt_bf = peer_v[0].reshape(-1, PEER_TE, D_MODEL).transpose(0, 2, 1).astype(BF16)

    proj = _in_proj(x2, row(norm_mix_w[0]), w_in_pad)
    ycat = _mixer(proj, bsz, seq,
                  lru_conv_w[0].astype(F32), row(lru_conv_b[0]), wg, row(lru_ba[0]), row(lru_bx[0]),
                  row(lru_lambda[0]), ssd_conv_w[0].astype(F32), row(ssd_conv_b[0]),
                  pad_heads(ssd_dt_bias[0]), pad_heads(ssd_a_log[0]), dexp, row(ssd_norm_w[0]), expand)
    x1, hnt, s1, s2 = _scores(ycat, x2, w_out[0].astype(BF16), row(norm_ffn_w[0]), wq, k1, k2)
    cnt, a1, rank, a2 = _router(s1, s2)
    out = _peer(hnt, u_bf, vt_bf, rank, a2, cnt, a1, x1, row(norm_final_w))
    return out.reshape(bsz, seq, D_MODEL)
```

```python
import functools
import math

import jax
import jax.numpy as jnp
from jax import lax
from jax.experimental import pallas as pl
from jax.experimental.pallas import tpu as pltpu

F32 = jnp.float32
BF16 = jnp.bfloat16

D_MODEL = 2048
CONV_WIDTH = 4
LRU_WIDTH = 1024
LRU_HEADS = 16
LRU_HEAD_DIM = 64
LRU_C = 8.0
SSD_WIDTH = 1024
SSD_HEAD_DIM = 64
SSD_HEADS = 16
SSD_GROUPS = 2
SSD_STATE = 128
SSD_CONV_DIM = SSD_WIDTH + 2 * SSD_GROUPS * SSD_STATE
PEER_HEADS = 8
PEER_N_KEYS = 128
PEER_HALF = 128
PEER_TOPK = 16
EPS = 1e-6

LANES = 128
SUBLANES = 8
MXU_WIDTH = 256
DT_PAD = LANES
PROJ_PAD = 3 * 1024 + SSD_CONV_DIM + DT_PAD
GATE_CHUNK = 256

IN_TILE = 256
SEQ_TILE = 256
SSD_CHUNK = 128
SCORE_TILE = 256
ROUTER_TILE = 128
PEER_TQ = 512
PEER_TE = 1024
GATE_KEYS = 2
PEER_SKEW = 1
VMEM_LIMIT = 56 * 1024 * 1024
PITCH_PAD = LANES


def _rms(x, w):
    return x * lax.rsqrt(jnp.mean(x * x, axis=-1, keepdims=True) + EPS) * w


def _in_proj_kernel(x_ref, nw_ref, w_ref, o_ref):
    hn = _rms(x_ref[...], nw_ref[...])
    o_ref[...] = jnp.dot(hn.astype(BF16), w_ref[...], preferred_element_type=F32)


def _in_proj(x2, norm_w, w_in_pad):
    t = x2.shape[0]
    return pl.pallas_call(
        _in_proj_kernel,
        grid=(t // IN_TILE,),
        in_specs=[
            pl.BlockSpec((IN_TILE, D_MODEL), lambda i: (i, 0)),
            pl.BlockSpec((1, D_MODEL), lambda i: (0, 0)),
            pl.BlockSpec((D_MODEL, PROJ_PAD), lambda i: (0, 0), pipeline_mode=pl.Buffered(1)),
        ],
        out_specs=pl.BlockSpec((IN_TILE, PROJ_PAD), lambda i: (i, 0)),
        out_shape=jax.ShapeDtypeStruct((t, PROJ_PAD), F32),
        compiler_params=pltpu.CompilerParams(
            dimension_semantics=("arbitrary",), vmem_limit_bytes=VMEM_LIMIT),
        name="in_proj",
    )(x2, norm_w, w_in_pad)


def _shift_rows(x, d, fill):
    rolled = pltpu.roll(x, d, axis=0)
    row = lax.broadcasted_iota(jnp.int32, x.shape, 0)
    return jnp.where(row < d, fill, rolled)


def _causal_conv(cur, prev8, w_ref, b_ref):
    row8 = lax.broadcasted_iota(jnp.int32, prev8.shape, 0)
    out = cur * w_ref[CONV_WIDTH - 1:CONV_WIDTH, :] + b_ref[...]
    for d in range(1, CONV_WIDTH):
        rolled = pltpu.roll(cur, d, axis=0)
        top = jnp.where(row8 < d, pltpu.roll(prev8, d, axis=0), rolled[:SUBLANES])
        shifted = jnp.concatenate([top, rolled[SUBLANES:]], axis=0)
        k = CONV_WIDTH - 1 - d
        out = out + shifted * w_ref[k:k + 1, :]
    return out


def _mixer_kernel(lx_ref, lg_ref, z_ref, xbc_ref, dt_ref,
                  lcw_ref, lcb_ref, wg_ref, ba_ref, bx_ref, lam_ref,
                  scw_ref, scb_ref, dtb_ref, alog_ref, dexp_ref, nw_ref, expand_ref,
                  y_ref,
                  lprev_ref, sprev_ref, h_ref, state_ref):
    ts = lx_ref.shape[0]

    @pl.when(pl.program_id(1) == 0)
    def _():
        lprev_ref[...] = jnp.zeros_like(lprev_ref)
        sprev_ref[...] = jnp.zeros_like(sprev_ref)
        h_ref[...] = jnp.zeros_like(h_ref)
        state_ref[...] = jnp.zeros_like(state_ref)

    lx = lx_ref[...]
    xl = _causal_conv(lx, lprev_ref[...], lcw_ref, lcb_ref)
    lprev_ref[...] = lx[ts - SUBLANES:, :]
    xl_bf = xl.astype(BF16)
    ra, ri = [], []
    for c in range(LRU_WIDTH // GATE_CHUNK):
        g = jnp.dot(xl_bf[:, c * GATE_CHUNK:(c + 1) * GATE_CHUNK], wg_ref[c],
                    preferred_element_type=F32)
        ra.append(g[:, :GATE_CHUNK])
        ri.append(g[:, GATE_CHUNK:])
    r = jax.nn.sigmoid(jnp.concatenate(ra, axis=1) + ba_ref[...])
    i = jax.nn.sigmoid(jnp.concatenate(ri, axis=1) + bx_ref[...])
    log_a = -LRU_C * r * jax.nn.softplus(-lam_ref[...])
    a = jnp.exp(log_a)
    b2 = jnp.tanh(-log_a) * (1.0 + a * a)
    b = jnp.where(b2 > 0.0, b2 * lax.rsqrt(b2), 0.0) * (i * xl)
    in_group = lax.broadcasted_iota(jnp.int32, a.shape, 0) % SUBLANES
    d = 1
    while d < SUBLANES:
        b = a * jnp.where(in_group < d, 0.0, pltpu.roll(b, d, axis=0)) + b
        a = a * jnp.where(in_group < d, 1.0, pltpu.roll(a, d, axis=0))
        d *= 2
    h = h_ref[0:1, :]
    groups = []
    for g in range(ts // SUBLANES):
        rows = slice(g * SUBLANES, (g + 1) * SUBLANES)
        hg = a[rows] * h + b[rows]
        groups.append(hg)
        h = hg[SUBLANES - 1:SUBLANES, :]
    hs = jnp.concatenate(groups, axis=0)
    h_ref[...] = jnp.broadcast_to(h, h_ref.shape)
    y_ref[:, :LRU_WIDTH] = (hs * jax.nn.gelu(lg_ref[...])).astype(y_ref.dtype)

    xbc_raw = xbc_ref[...]
    xbc = jax.nn.silu(_causal_conv(xbc_raw, sprev_ref[...], scw_ref, scb_ref))
    sprev_ref[...] = xbc_raw[ts - SUBLANES:, :]
    xs = xbc[:, :SSD_WIDTH]
    dt = jax.nn.softplus(dt_ref[...] + dtb_ref[...])
    adt = -jnp.exp(alog_ref[...]) * dt
    expand = expand_ref[...]

    def widen(v):
        return jnp.dot(v, expand, precision=lax.Precision.HIGHEST, preferred_element_type=F32)

    xc = xs * widen(dt)
    tri = (lax.broadcasted_iota(jnp.int32, (SSD_CHUNK, SSD_CHUNK), 0)
           >= lax.broadcasted_iota(jnp.int32, (SSD_CHUNK, SSD_CHUNK), 1))
    hpg = SSD_HEADS // SSD_GROUPS
    gw = hpg * SSD_HEAD_DIM
    y_chunks = []
    for c in range(ts // SSD_CHUNK):
        rows = slice(c * SSD_CHUNK, (c + 1) * SSD_CHUNK)
        acs = adt[rows]
        d = 1
        while d < SSD_CHUNK:
            acs = acs + _shift_rows(acs, d, 0.0)
            d *= 2
        acs_t = acs.T
        total = acs[SSD_CHUNK - 1:SSD_CHUNK, :]
        e_in = widen(jnp.exp(acs))
        e_out = widen(jnp.exp(total - acs))
        e_all = widen(jnp.exp(total) * jnp.ones((SUBLANES, 1), F32))[0:1, :]
        xc_c = xc[rows]
        xdec = (xc_c * e_out).astype(BF16)
        xc_bf = xc_c.astype(BF16)
        y_parts = []
        for g in range(SSD_GROUPS):
            bg = xbc[rows, SSD_WIDTH + g * SSD_STATE:SSD_WIDTH + (g + 1) * SSD_STATE].astype(BF16)
            cg = xbc[rows, SSD_WIDTH + (SSD_GROUPS + g) * SSD_STATE:
                     SSD_WIDTH + (SSD_GROUPS + g + 1) * SSD_STATE].astype(BF16)
            scores = lax.dot_general(cg, bg, (((1,), (1,)), ((), ())), preferred_element_type=F32)
            gs = slice(g * gw, (g + 1) * gw)
            st = state_ref[:, gs]
            y_off = jnp.dot(cg, st.astype(BF16), preferred_element_type=F32) * e_in[:, gs]
            diag = []
            for hh in range(hpg):
                h = g * hpg + hh
                lm = jnp.where(tri, jnp.exp(acs[:, h:h + 1] - acs_t[h:h + 1, :]), 0.0)
                p = (scores * lm).astype(BF16)
                diag.append(jnp.dot(p, xc_bf[:, h * SSD_HEAD_DIM:(h + 1) * SSD_HEAD_DIM],
                                    preferred_element_type=F32))
            y_parts.append(jnp.concatenate(diag, axis=1) + y_off)
            new = lax.dot_general(bg, xdec[:, gs], (((0,), (0,)), ((), ())),
                                  preferred_element_type=F32)
            state_ref[:, gs] = st * e_all[:, gs] + new
        y_chunks.append(jnp.concatenate(y_parts, axis=1))
    y = jnp.concatenate(y_chunks, axis=0) + dexp_ref[...] * xs
    yg = y * jax.nn.silu(z_ref[...])
    y_ref[:, LRU_WIDTH:] = _rms(yg, nw_ref[...]).astype(y_ref.dtype)


def _mixer(proj, bsz, seq, lcw, lcb, wg, ba, bx, lam, scw, scb, dtb, alog, dexp, nw, expand):
    nj = seq // SEQ_TILE

    def col(width, idx):
        return pl.BlockSpec((SEQ_TILE, width), lambda b, j: (b * nj + j, idx))

    def full(arr):
        nd = arr.ndim
        return pl.BlockSpec(arr.shape, lambda b, j: (0,) * nd)

    consts = (lcw, lcb, wg, ba, bx, lam, scw, scb, dtb, alog, dexp, nw, expand)
    return pl.pallas_call(
        _mixer_kernel,
        grid=(bsz, nj),
        in_specs=[col(1024, 0), col(1024, 1), col(1024, 2), col(SSD_CONV_DIM, 2),
                  col(DT_PAD, (3 * 1024 + SSD_CONV_DIM) // DT_PAD)] + [full(c) for c in consts],
        out_specs=pl.BlockSpec((SEQ_TILE, 2 * 1024), lambda b, j: (b * nj + j, 0)),
        out_shape=jax.ShapeDtypeStruct((bsz * seq, 2 * 1024), BF16),
        scratch_shapes=[
            pltpu.VMEM((SUBLANES, LRU_WIDTH), F32),
            pltpu.VMEM((SUBLANES, SSD_CONV_DIM), F32),
            pltpu.VMEM((SUBLANES, LRU_WIDTH), F32),
            pltpu.VMEM((SSD_STATE, SSD_WIDTH), F32),
        ],
        compiler_params=pltpu.CompilerParams(
            dimension_semantics=("arbitrary", "arbitrary"), vmem_limit_bytes=VMEM_LIMIT),
        name="mixer",
    )(proj, proj, proj, proj, proj, *consts)


def _sort_desc(v):
    v = list(v)
    n = len(v)
    k = 2
    while k <= n:
        j = k // 2
        while j >= 1:
            for i in range(n):
                l = i ^ j
                if l > i:
                    hi, lo = jnp.maximum(v[i], v[l]), jnp.minimum(v[i], v[l])
                    v[i], v[l] = (hi, lo) if (i & k) == 0 else (lo, hi)
            j //= 2
        k *= 2
    return v


def _merge_top(a, b):
    n = len(a)
    v = [jnp.maximum(a[i], b[n - 1 - i]) for i in range(n)]
    j = n // 2
    while j >= 1:
        for i in range(n):
            l = i ^ j
            if l > i:
                v[i], v[l] = jnp.maximum(v[i], v[l]), jnp.minimum(v[i], v[l])
        j //= 2
    return v


def _top_sorted(vals):
    groups = [_sort_desc(vals[i:i + PEER_TOPK]) for i in range(0, len(vals), PEER_TOPK)]
    while len(groups) > 1:
        nxt = [_merge_top(groups[i], groups[i + 1]) for i in range(0, len(groups) - 1, 2)]
        if len(groups) % 2:
            nxt.append(groups[-1])
        groups = nxt
    return groups[0]


def _scores_kernel(y_ref, x_ref, wout_ref, nw_ref, wq_ref, k1_ref, k2_ref,
                   x1_ref, hnt_ref, s1_ref, s2_ref):
    x1 = x_ref[...] + jnp.dot(y_ref[...], wout_ref[...], preferred_element_type=F32)
    x1_ref[...] = x1
    hn = _rms(x1, nw_ref[...]).astype(BF16)
    hnt_ref[...] = hn.T
    q = jnp.dot(hn, wq_ref[...], preferred_element_type=F32).astype(BF16)
    half = PEER_HEADS * PEER_HALF
    nt = (((1,), (1,)), ((), ()))
    s1_ref[...] = lax.dot_general(k1_ref[...], q[:, :half], nt, preferred_element_type=F32)
    s2_ref[...] = lax.dot_general(k2_ref[...], q[:, half:], nt, preferred_element_type=F32)


def _scores(ycat, x2, w_out, nw, wq, k1, k2):
    t = x2.shape[0]
    nrow = PEER_HEADS * PEER_N_KEYS
    tok = pl.BlockSpec((SCORE_TILE, D_MODEL), lambda i: (i, 0))
    tab = pl.BlockSpec((nrow, SCORE_TILE), lambda i: (0, i))

    def full(arr):
        nd = arr.ndim
        return pl.BlockSpec(arr.shape, lambda i: (0,) * nd, pipeline_mode=pl.Buffered(1))

    return pl.pallas_call(
        _scores_kernel,
        grid=(t // SCORE_TILE,),
        in_specs=[tok, tok, full(w_out), full(nw), full(wq), full(k1), full(k2)],
        out_specs=[tok, pl.BlockSpec((D_MODEL, SCORE_TILE), lambda i: (0, i)), tab, tab],
        out_shape=[
            jax.ShapeDtypeStruct((t, D_MODEL), F32),
            jax.ShapeDtypeStruct((D_MODEL, t), BF16),
            jax.ShapeDtypeStruct((nrow, t), F32),
            jax.ShapeDtypeStruct((nrow, t), F32),
        ],
        compiler_params=pltpu.CompilerParams(
            dimension_semantics=("arbitrary",), vmem_limit_bytes=VMEM_LIMIT),
        name="scores",
    )(ycat, x2, w_out, nw, wq, k1, k2)


def _router_kernel(s1_ref, s2_ref, cnt_ref, a1_ref, rank_ref, a2_ref):
    tq = s1_ref.shape[1]
    s1 = s1_ref[...].reshape(PEER_N_KEYS, PEER_HEADS, tq)
    s2 = s2_ref[...]

    a = _top_sorted([s1[n] for n in range(PEER_N_KEYS)])
    b = _top_sorted([s2_ref[pl.ds(n, PEER_HEADS, stride=PEER_N_KEYS), :] for n in range(PEER_N_KEYS)])
    cands = [a[i] + b[j] for i in range(PEER_TOPK) for j in range(PEER_TOPK)
             if (i + 1) * (j + 1) <= PEER_TOPK]
    neg = jnp.full_like(a[0], -jnp.inf)
    tau = _top_sorted(cands + [neg] * (-len(cands) % PEER_TOPK))[PEER_TOPK - 1]
    m = a[0] + b[0]
    z = jnp.zeros_like(m)
    for c in cands:
        z = z + jnp.where(c >= tau, jnp.exp(c - m), 0.0)
    inv_z = 1.0 / z

    cnt = jnp.zeros_like(s1)
    for j in range(PEER_TOPK):
        cnt = cnt + jnp.where(s1 + b[j] >= tau, 1.0, 0.0)
    cnt_ref[...] = cnt.reshape(PEER_N_KEYS * PEER_HEADS, tq)
    a1_ref[...] = jnp.exp(s1 - a[0]).reshape(PEER_N_KEYS * PEER_HEADS, tq)
    for h in range(PEER_HEADS):
        rows = slice(h * PEER_N_KEYS, (h + 1) * PEER_N_KEYS)
        s2h = s2[rows, :]
        rank = jnp.zeros_like(s2h)
        for k in range(PEER_TOPK):
            rank = rank + jnp.where(b[k][h:h + 1, :] > s2h, 1.0, 0.0)
        rank_ref[rows, :] = rank.astype(rank_ref.dtype)
        a2_ref[rows, :] = (jnp.exp(s2h - b[0][h:h + 1, :]) * inv_z[h:h + 1, :]).astype(a2_ref.dtype)


def _router(s1, s2):
    nrow, t = s1.shape
    tab = pl.BlockSpec((nrow, ROUTER_TILE), lambda i: (0, i))
    return pl.pallas_call(
        _router_kernel,
        grid=(t // ROUTER_TILE,),
        in_specs=[tab, tab],
        out_specs=[tab, tab, tab, tab],
        out_shape=[jax.ShapeDtypeStruct((nrow, t), F32)] * 4,
        compiler_params=pltpu.CompilerParams(
            dimension_semantics=("arbitrary",), vmem_limit_bytes=VMEM_LIMIT),
        name="router",
    )(s1, s2)


def _peer_kernel(hnt_ref, u_ref, vt_ref, rank_ref, a2_ref, cnt_ref, a1_ref, x1_ref, nw_ref,
                 o_ref, acc_ref, act0_ref, act1_ref):
    s = pl.program_id(1)
    n_steps = pl.num_programs(1)

    @pl.when((pl.program_id(0) == 0) & (s == 0))
    def _():
        act1_ref[...] = jnp.zeros_like(act1_ref)
        acc_ref[...] = jnp.zeros_like(acc_ref)

    @pl.when(s == PEER_SKEW)
    def _():
        acc_ref[...] = jnp.zeros_like(acc_ref)

    def gated(act_r, js, c):
        lanes = slice(c * LANES, (c + 1) * LANES)
        w = [None] * len(js)
        for h in range(PEER_HEADS):
            rows = slice(h * PEER_N_KEYS, (h + 1) * PEER_N_KEYS)
            rank_h, a2_h = rank_ref[rows, lanes], a2_ref[rows, lanes]
            for k, j in enumerate(js):
                r = j * PEER_HEADS + h
                term = jnp.where(rank_h < cnt_ref[r:r + 1, lanes], a2_h * a1_ref[r:r + 1, lanes], 0.0)
                w[k] = term if w[k] is None else w[k] + term
        return [(jax.nn.gelu(act_r[j * PEER_N_KEYS:(j + 1) * PEER_N_KEYS, lanes]) * w[k]).astype(BF16)
                for k, j in enumerate(js)]

    def stages(act_r, act_w):
        per = MXU_WIDTH // LANES
        n_keys = PEER_TE // PEER_N_KEYS
        for p in range(PEER_TQ // MXU_WIDTH):
            lanes = slice(p * MXU_WIDTH, (p + 1) * MXU_WIDTH)
            cols = []
            for cc in range(per):
                blocks = []
                for j0 in range(0, n_keys, GATE_KEYS):
                    blocks += gated(act_r, range(j0, j0 + GATE_KEYS), p * per + cc)
                cols.append(jnp.concatenate(blocks, axis=0))
            wact = jnp.concatenate(cols, axis=1)
            act_w[:, lanes] = jnp.dot(u_ref[:, :D_MODEL], hnt_ref[:, lanes], preferred_element_type=F32)
            acc_ref[:, lanes] += jnp.dot(vt_ref[:, :PEER_TE], wact, preferred_element_type=F32)

    @pl.when(s % 2 == 0)
    def _():
        stages(act1_ref, act0_ref)

    @pl.when(s % 2 == 1)
    def _():
        stages(act0_ref, act1_ref)

    @pl.when(s == n_steps - 1)
    def _():
        o_ref[...] = _rms(x1_ref[...] + acc_ref[...].T, nw_ref[...])


def _peer(hnt, u_bf, vt_bf, rank, a2, cnt, a1, x1, nw):
    t = x1.shape[0]
    n_exp = u_bf.shape[0]
    nrow = PEER_HEADS * PEER_N_KEYS
    rows_per_step = PEER_HEADS * (PEER_TE // PEER_N_KEYS)
    n_tiles = n_exp // PEER_TE
    tile = lambda s, lag: jnp.clip(s - lag, 0, n_tiles - 1)
    tok_tab = pl.BlockSpec((nrow, PEER_TQ), lambda i, s: (0, i))
    exp_tab = pl.BlockSpec((rows_per_step, PEER_TQ), lambda i, s: (tile(s, PEER_SKEW), i))
    tok = pl.BlockSpec((PEER_TQ, D_MODEL), lambda i, s: (i, 0))
    once = lambda spec: pl.BlockSpec(spec.block_shape, spec.index_map, pipeline_mode=pl.Buffered(1))
    return pl.pallas_call(
        _peer_kernel,
        grid=(t // PEER_TQ, n_tiles + PEER_SKEW),
        in_specs=[
            pl.BlockSpec((D_MODEL, PEER_TQ), lambda i, s: (0, i)),
            pl.BlockSpec((PEER_TE, D_MODEL + PITCH_PAD), lambda i, s: (tile(s, 0), 0)),
            pl.BlockSpec((None, D_MODEL, PEER_TE + PITCH_PAD), lambda i, s: (tile(s, PEER_SKEW), 0, 0)),
            tok_tab, tok_tab, exp_tab, exp_tab, once(tok),
            pl.BlockSpec((1, D_MODEL), lambda i, s: (0, 0)),
        ],
        out_specs=tok,
        out_shape=jax.ShapeDtypeStruct((t, D_MODEL), F32),
        scratch_shapes=[pltpu.VMEM((D_MODEL, PEER_TQ), F32),
                        pltpu.VMEM((PEER_TE, PEER_TQ), F32),
                        pltpu.VMEM((PEER_TE, PEER_TQ), F32)],
        compiler_params=pltpu.CompilerParams(
            dimension_semantics=("arbitrary", "arbitrary"), vmem_limit_bytes=VMEM_LIMIT),
        name="peer",
    )(hnt, u_bf, vt_bf, rank, a2, cnt, a1, x1, nw)


def _block_diag_chunks(w):
    per = GATE_CHUNK // LRU_HEAD_DIM
    w4 = w.reshape(LRU_HEADS // per, per, LRU_HEAD_DIM, LRU_HEAD_DIM)
    bd = jnp.einsum('cgij,gk->cgikj', w4, jnp.eye(per, dtype=w.dtype))
    return bd.reshape(LRU_HEADS // per, GATE_CHUNK, GATE_CHUNK)


def kernel(x, norm_mix_w, w_in, lru_conv_w, lru_conv_b, lru_wa, lru_ba, lru_wx, lru_bx, lru_lambda, ssd_conv_w, ssd_conv_b, ssd_dt_bias, ssd_a_log, ssd_d, ssd_norm_w, w_out, norm_ffn_w, peer_wq, peer_sub_keys, peer_u, peer_v, norm_final_w):
    bsz, seq, _ = x.shape
    assert norm_mix_w.shape[0] == 1, "one layer"
    t = bsz * seq
    assert seq % SEQ_TILE == 0 and t % PEER_TQ == 0 and t % IN_TILE == 0 and t % SCORE_TILE == 0
    x2 = x.reshape(t, D_MODEL)
    row = lambda v: v.reshape(1, -1).astype(F32)

    w_in_pad = jnp.pad(w_in[0], ((0, 0), (0, PROJ_PAD - w_in.shape[-1]))).astype(BF16)
    wg = jnp.concatenate([_block_diag_chunks(lru_wa[0]), _block_diag_chunks(lru_wx[0])], axis=-1).astype(BF16)
    pad_heads = lambda v: jnp.pad(v.reshape(1, -1).astype(F32), ((0, 0), (0, DT_PAD - SSD_HEADS)))
    expand = (jnp.arange(DT_PAD)[:, None] == (jnp.arange(SSD_WIDTH) // SSD_HEAD_DIM)[None, :]).astype(F32)
    dexp = jnp.repeat(ssd_d[0].astype(F32), SSD_HEAD_DIM).reshape(1, -1)
    wq = peer_wq[0].reshape(D_MODEL, PEER_HEADS, 2, PEER_HALF).transpose(0, 2, 1, 3)
    wq = wq.reshape(D_MODEL, 2 * PEER_HEADS * PEER_HALF).astype(BF16)
    keys = peer_sub_keys[0]
    eye = jnp.eye(PEER_HEADS, dtype=keys.dtype)
    k1 = jnp.einsum('hnd,hg->nhgd', keys[:, 0], eye).reshape(PEER_N_KEYS * PEER_HEADS, -1).astype(BF16)
    k2 = jnp.einsum('hnd,hg->hngd', keys[:, 1], eye).reshape(PEER_HEADS * PEER_N_KEYS, -1).astype(BF16)
    pad_lanes = lambda v: jnp.pad(v, [(0, 0)] * (v.ndim - 1) + [(0, PITCH_PAD)])
    u_bf = pad_lanes(peer_u[0].astype(BF16))
    vt_bf = pad_lanes(peer_v[0].reshape(-1, PEER_TE, D_MODEL).transpose(0, 2, 1).astype(BF16))

    proj = _in_proj(x2, row(norm_mix_w[0]), w_in_pad)
    ycat = _mixer(proj, bsz, seq,
                  lru_conv_w[0].astype(F32), row(lru_conv_b[0]), wg, row(lru_ba[0]), row(lru_bx[0]),
                  row(lru_lambda[0]), ssd_conv_w[0].astype(F32), row(ssd_conv_b[0]),
                  pad_heads(ssd_dt_bias[0]), pad_heads(ssd_a_log[0]), dexp, row(ssd_norm_w[0]), expand)
    x1, hnt, s1, s2 = _scores(ycat, x2, w_out[0].astype(BF16), row(norm_ffn_w[0]), wq, k1, k2)
    cnt, a1, rank, a2 = _router(s1, s2)
    out = _peer(hnt, u_bf, vt_bf, rank, a2, cnt, a1, x1, row(norm_final_w))
    return out.reshape(bsz, seq, D_MODEL)
```

```python
import functools
import math

import jax
import jax.numpy as jnp
from jax import lax
from jax.experimental import pallas as pl
from jax.experimental.pallas import tpu as pltpu

F32 = jnp.float32
BF16 = jnp.bfloat16

D_MODEL = 2048
CONV_WIDTH = 4
LRU_WIDTH = 1024
LRU_HEADS = 16
LRU_HEAD_DIM = 64
LRU_C = 8.0
SSD_WIDTH = 1024
SSD_HEAD_DIM = 64
SSD_HEADS = 16
SSD_GROUPS = 2
SSD_STATE = 128
SSD_CONV_DIM = SSD_WIDTH + 2 * SSD_GROUPS * SSD_STATE
PEER_HEADS = 8
PEER_N_KEYS = 128
PEER_HALF = 128
PEER_TOPK = 16
EPS = 1e-6

LANES = 128
SUBLANES = 8
MXU_WIDTH = 256
DT_PAD = LANES
PROJ_PAD = 3 * 1024 + SSD_CONV_DIM + DT_PAD
GATE_CHUNK = 256

IN_TILE = 256
SEQ_TILE = 256
SSD_CHUNK = 128
SCORE_TILE = 256
ROUTER_TILE = 128
PEER_TQ = 512
PEER_TE = 1024
GATE_KEYS = 2
PEER_SKEW = 1
VMEM_LIMIT = 56 * 1024 * 1024
PITCH_PAD = LANES


def _rms(x, w):
    return x * lax.rsqrt(jnp.mean(x * x, axis=-1, keepdims=True) + EPS) * w


def _in_proj_kernel(x_ref, nw_ref, w_ref, o_ref):
    hn = _rms(x_ref[...], nw_ref[...])
    o_ref[...] = jnp.dot(hn.astype(BF16), w_ref[...], preferred_element_type=F32)


def _in_proj(x2, norm_w, w_in_pad):
    t = x2.shape[0]
    return pl.pallas_call(
        _in_proj_kernel,
        grid=(t // IN_TILE,),
        in_specs=[
            pl.BlockSpec((IN_TILE, D_MODEL), lambda i: (i, 0)),
            pl.BlockSpec((1, D_MODEL), lambda i: (0, 0)),
            pl.BlockSpec((D_MODEL, PROJ_PAD), lambda i: (0, 0), pipeline_mode=pl.Buffered(1)),
        ],
        out_specs=pl.BlockSpec((IN_TILE, PROJ_PAD), lambda i: (i, 0)),
        out_shape=jax.ShapeDtypeStruct((t, PROJ_PAD), F32),
        compiler_params=pltpu.CompilerParams(
            dimension_semantics=("arbitrary",), vmem_limit_bytes=VMEM_LIMIT),
        name="in_proj",
    )(x2, norm_w, w_in_pad)


def _shift_rows(x, d, fill):
    rolled = pltpu.roll(x, d, axis=0)
    row = lax.broadcasted_iota(jnp.int32, x.shape, 0)
    return jnp.where(row < d, fill, rolled)


def _causal_conv(cur, prev8, w_ref, b_ref):
    row8 = lax.broadcasted_iota(jnp.int32, prev8.shape, 0)
    out = cur * w_ref[CONV_WIDTH - 1:CONV_WIDTH, :] + b_ref[...]
    for d in range(1, CONV_WIDTH):
        rolled = pltpu.roll(cur, d, axis=0)
        top = jnp.where(row8 < d, pltpu.roll(prev8, d, axis=0), rolled[:SUBLANES])
        shifted = jnp.concatenate([top, rolled[SUBLANES:]], axis=0)
        k = CONV_WIDTH - 1 - d
        out = out + shifted * w_ref[k:k + 1, :]
    return out


def _mixer_kernel(lx_ref, lg_ref, z_ref, xbc_ref, dt_ref,
                  lcw_ref, lcb_ref, wg_ref, ba_ref, bx_ref, lam_ref,
                  scw_ref, scb_ref, dtb_ref, alog_ref, dexp_ref, nw_ref, expand_ref,
                  y_ref,
                  lprev_ref, sprev_ref, h_ref, state_ref):
    ts = lx_ref.shape[0]

    @pl.when(pl.program_id(1) == 0)
    def _():
        lprev_ref[...] = jnp.zeros_like(lprev_ref)
        sprev_ref[...] = jnp.zeros_like(sprev_ref)
        h_ref[...] = jnp.zeros_like(h_ref)
        state_ref[...] = jnp.zeros_like(state_ref)

    lx = lx_ref[...]
    xl = _causal_conv(lx, lprev_ref[...], lcw_ref, lcb_ref)
    lprev_ref[...] = lx[ts - SUBLANES:, :]
    xl_bf = xl.astype(BF16)
    ra, ri = [], []
    for c in range(LRU_WIDTH // GATE_CHUNK):
        g = jnp.dot(xl_bf[:, c * GATE_CHUNK:(c + 1) * GATE_CHUNK], wg_ref[c],
                    preferred_element_type=F32)
        ra.append(g[:, :GATE_CHUNK])
        ri.append(g[:, GATE_CHUNK:])
    r = jax.nn.sigmoid(jnp.concatenate(ra, axis=1) + ba_ref[...])
    i = jax.nn.sigmoid(jnp.concatenate(ri, axis=1) + bx_ref[...])
    log_a = -LRU_C * r * jax.nn.softplus(-lam_ref[...])
    a = jnp.exp(log_a)
    b2 = jnp.tanh(-log_a) * (1.0 + a * a)
    b = jnp.where(b2 > 0.0, b2 * lax.rsqrt(b2), 0.0) * (i * xl)
    in_group = lax.broadcasted_iota(jnp.int32, a.shape, 0) % SUBLANES
    d = 1
    while d < SUBLANES:
        b = a * jnp.where(in_group < d, 0.0, pltpu.roll(b, d, axis=0)) + b
        a = a * jnp.where(in_group < d, 1.0, pltpu.roll(a, d, axis=0))
        d *= 2
    h = h_ref[0:1, :]
    groups = []
    for g in range(ts // SUBLANES):
        rows = slice(g * SUBLANES, (g + 1) * SUBLANES)
        hg = a[rows] * h + b[rows]
        groups.append(hg)
        h = hg[SUBLANES - 1:SUBLANES, :]
    hs = jnp.concatenate(groups, axis=0)
    h_ref[...] = jnp.broadcast_to(h, h_ref.shape)
    y_ref[:, :LRU_WIDTH] = (hs * jax.nn.gelu(lg_ref[...])).astype(y_ref.dtype)

    xbc_raw = xbc_ref[...]
    xbc = jax.nn.silu(_causal_conv(xbc_raw, sprev_ref[...], scw_ref, scb_ref))
    sprev_ref[...] = xbc_raw[ts - SUBLANES:, :]
    xs = xbc[:, :SSD_WIDTH]
    dt = jax.nn.softplus(dt_ref[...] + dtb_ref[...])
    adt = -jnp.exp(alog_ref[...]) * dt
    expand = expand_ref[...]

    def widen(v):
        return jnp.dot(v, expand, precision=lax.Precision.HIGHEST, preferred_element_type=F32)

    xc = xs * widen(dt)
    tri = (lax.broadcasted_iota(jnp.int32, (SSD_CHUNK, SSD_CHUNK), 0)
           >= lax.broadcasted_iota(jnp.int32, (SSD_CHUNK, SSD_CHUNK), 1))
    hpg = SSD_HEADS // SSD_GROUPS
    gw = hpg * SSD_HEAD_DIM
    y_chunks = []
    for c in range(ts // SSD_CHUNK):
        rows = slice(c * SSD_CHUNK, (c + 1) * SSD_CHUNK)
        acs = adt[rows]
        d = 1
        while d < SSD_CHUNK:
            acs = acs + _shift_rows(acs, d, 0.0)
            d *= 2
        acs_t = acs.T
        total = acs[SSD_CHUNK - 1:SSD_CHUNK, :]
        e_in = widen(jnp.exp(acs))
        e_out = widen(jnp.exp(total - acs))
        e_all = widen(jnp.exp(total) * jnp.ones((SUBLANES, 1), F32))[0:1, :]
        xc_c = xc[rows]
        xdec = (xc_c * e_out).astype(BF16)
        xc_bf = xc_c.astype(BF16)
        y_parts = []
        for g in range(SSD_GROUPS):
            bg = xbc[rows, SSD_WIDTH + g * SSD_STATE:SSD_WIDTH + (g + 1) * SSD_STATE].astype(BF16)
            cg = xbc[rows, SSD_WIDTH + (SSD_GROUPS + g) * SSD_STATE:
                     SSD_WIDTH + (SSD_GROUPS + g + 1) * SSD_STATE].astype(BF16)
            scores = lax.dot_general(cg, bg, (((1,), (1,)), ((), ())), preferred_element_type=F32)
            gs = slice(g * gw, (g + 1) * gw)
            st = state_ref[:, gs]
            y_off = jnp.dot(cg, st.astype(BF16), preferred_element_type=F32) * e_in[:, gs]
            diag = []
            for hh in range(hpg):
                h = g * hpg + hh
                lm = jnp.where(tri, jnp.exp(acs[:, h:h + 1] - acs_t[h:h + 1, :]), 0.0)
                p = (scores * lm).astype(BF16)
                diag.append(jnp.dot(p, xc_bf[:, h * SSD_HEAD_DIM:(h + 1) * SSD_HEAD_DIM],
                                    preferred_element_type=F32))
            y_parts.append(jnp.concatenate(diag, axis=1) + y_off)
            new = lax.dot_general(bg, xdec[:, gs], (((0,), (0,)), ((), ())),
                                  preferred_element_type=F32)
            state_ref[:, gs] = st * e_all[:, gs] + new
        y_chunks.append(jnp.concatenate(y_parts, axis=1))
    y = jnp.concatenate(y_chunks, axis=0) + dexp_ref[...] * xs
    yg = y * jax.nn.silu(z_ref[...])
    y_ref[:, LRU_WIDTH:] = _rms(yg, nw_ref[...]).astype(y_ref.dtype)


def _mixer(proj, bsz, seq, lcw, lcb, wg, ba, bx, lam, scw, scb, dtb, alog, dexp, nw, expand):
    nj = seq // SEQ_TILE

    def col(width, idx):
        return pl.BlockSpec((SEQ_TILE, width), lambda b, j: (b * nj + j, idx))

    def full(arr):
        nd = arr.ndim
        return pl.BlockSpec(arr.shape, lambda b, j: (0,) * nd)

    consts = (lcw, lcb, wg, ba, bx, lam, scw, scb, dtb, alog, dexp, nw, expand)
    return pl.pallas_call(
        _mixer_kernel,
        grid=(bsz, nj),
        in_specs=[col(1024, 0), col(1024, 1), col(1024, 2), col(SSD_CONV_DIM, 2),
                  col(DT_PAD, (3 * 1024 + SSD_CONV_DIM) // DT_PAD)] + [full(c) for c in consts],
        out_specs=pl.BlockSpec((SEQ_TILE, 2 * 1024), lambda b, j: (b * nj + j, 0)),
        out_shape=jax.ShapeDtypeStruct((bsz * seq, 2 * 1024), BF16),
        scratch_shapes=[
            pltpu.VMEM((SUBLANES, LRU_WIDTH), F32),
            pltpu.VMEM((SUBLANES, SSD_CONV_DIM), F32),
            pltpu.VMEM((SUBLANES, LRU_WIDTH), F32),
            pltpu.VMEM((SSD_STATE, SSD_WIDTH), F32),
        ],
        compiler_params=pltpu.CompilerParams(
            dimension_semantics=("arbitrary", "arbitrary"), vmem_limit_bytes=VMEM_LIMIT),
        name="mixer",
    )(proj, proj, proj, proj, proj, *consts)


def _sort_desc(v):
    v = list(v)
    n = len(v)
    k = 2
    while k <= n:
        j = k // 2
        while j >= 1:
            for i in range(n):
                l = i ^ j
                if l > i:
                    hi, lo = jnp.maximum(v[i], v[l]), jnp.minimum(v[i], v[l])
                    v[i], v[l] = (hi, lo) if (i & k) == 0 else (lo, hi)
            j //= 2
        k *= 2
    return v


def _merge_top(a, b):
    n = len(a)
    v = [jnp.maximum(a[i], b[n - 1 - i]) for i in range(n)]
    j = n // 2
    while j >= 1:
        for i in range(n):
            l = i ^ j
            if l > i:
                v[i], v[l] = jnp.maximum(v[i], v[l]), jnp.minimum(v[i], v[l])
        j //= 2
    return v


def _top_sorted(vals):
    groups = [_sort_desc(vals[i:i + PEER_TOPK]) for i in range(0, len(vals), PEER_TOPK)]
    while len(groups) > 1:
        nxt = [_merge_top(groups[i], groups[i + 1]) for i in range(0, len(groups) - 1, 2)]
        if len(groups) % 2:
            nxt.append(groups[-1])
        groups = nxt
    return groups[0]


def _scores_kernel(y_ref, x_ref, wout_ref, nw_ref, wq_ref, k1_ref, k2_ref,
                   x1_ref, hnt_ref, s1_ref, s2_ref):
    x1 = x_ref[...] + jnp.dot(y_ref[...], wout_ref[...], preferred_element_type=F32)
    x1_ref[...] = x1
    hn = _rms(x1, nw_ref[...]).astype(BF16)
    hnt_ref[...] = hn.T
    q = jnp.dot(hn, wq_ref[...], preferred_element_type=F32).astype(BF16)
    half = PEER_HEADS * PEER_HALF
    nt = (((1,), (1,)), ((), ()))
    s1_ref[...] = lax.dot_general(k1_ref[...], q[:, :half], nt, preferred_element_type=F32)
    s2_ref[...] = lax.dot_general(k2_ref[...], q[:, half:], nt, preferred_element_type=F32)


def _scores(ycat, x2, w_out, nw, wq, k1, k2):
    t = x2.shape[0]
    nrow = PEER_HEADS * PEER_N_KEYS
    tok = pl.BlockSpec((SCORE_TILE, D_MODEL), lambda i: (i, 0))
    tab = pl.BlockSpec((nrow, SCORE_TILE), lambda i: (0, i))

    def full(arr):
        nd = arr.ndim
        return pl.BlockSpec(arr.shape, lambda i: (0,) * nd, pipeline_mode=pl.Buffered(1))

    return pl.pallas_call(
        _scores_kernel,
        grid=(t // SCORE_TILE,),
        in_specs=[tok, tok, full(w_out), full(nw), full(wq), full(k1), full(k2)],
        out_specs=[tok, pl.BlockSpec((D_MODEL, SCORE_TILE), lambda i: (0, i)), tab, tab],
        out_shape=[
            jax.ShapeDtypeStruct((t, D_MODEL), F32),
            jax.ShapeDtypeStruct((D_MODEL, t), BF16),
            jax.ShapeDtypeStruct((nrow, t), F32),
            jax.ShapeDtypeStruct((nrow, t), F32),
        ],
        compiler_params=pltpu.CompilerParams(
            dimension_semantics=("arbitrary",), vmem_limit_bytes=VMEM_LIMIT),
        name="scores",
    )(ycat, x2, w_out, nw, wq, k1, k2)


def _router_kernel(s1_ref, s2_ref, cnt_ref, a1_ref, rank_ref, a2_ref):
    tq = s1_ref.shape[1]
    s1 = s1_ref[...].reshape(PEER_N_KEYS, PEER_HEADS, tq)
    s2 = s2_ref[...]

    a = _top_sorted([s1[n] for n in range(PEER_N_KEYS)])
    b = _top_sorted([s2_ref[pl.ds(n, PEER_HEADS, stride=PEER_N_KEYS), :] for n in range(PEER_N_KEYS)])
    cands = [a[i] + b[j] for i in range(PEER_TOPK) for j in range(PEER_TOPK)
             if (i + 1) * (j + 1) <= PEER_TOPK]
    neg = jnp.full_like(a[0], -jnp.inf)
    tau = _top_sorted(cands + [neg] * (-len(cands) % PEER_TOPK))[PEER_TOPK - 1]
    m = a[0] + b[0]
    z = jnp.zeros_like(m)
    for c in cands:
        z = z + jnp.where(c >= tau, jnp.exp(c - m), 0.0)
    inv_z = 1.0 / z

    cnt = jnp.zeros_like(s1)
    for j in range(PEER_TOPK):
        cnt = cnt + jnp.where(s1 + b[j] >= tau, 1.0, 0.0)
    cnt_ref[...] = cnt.reshape(PEER_N_KEYS * PEER_HEADS, tq)
    a1_ref[...] = jnp.exp(s1 - a[0]).reshape(PEER_N_KEYS * PEER_HEADS, tq)
    for h in range(PEER_HEADS):
        rows = slice(h * PEER_N_KEYS, (h + 1) * PEER_N_KEYS)
        s2h = s2[rows, :]
        rank = jnp.zeros_like(s2h)
        for k in range(PEER_TOPK):
            rank = rank + jnp.where(b[k][h:h + 1, :] > s2h, 1.0, 0.0)
        rank_ref[rows, :] = rank.astype(rank_ref.dtype)
        a2_ref[rows, :] = (jnp.exp(s2h - b[0][h:h + 1, :]) * inv_z[h:h + 1, :]).astype(a2_ref.dtype)


def _router(s1, s2):
    nrow, t = s1.shape
    tab = pl.BlockSpec((nrow, ROUTER_TILE), lambda i: (0, i))
    return pl.pallas_call(
        _router_kernel,
        grid=(t // ROUTER_TILE,),
        in_specs=[tab, tab],
        out_specs=[tab, tab, tab, tab],
        out_shape=[jax.ShapeDtypeStruct((nrow, t), F32)] * 4,
        compiler_params=pltpu.CompilerParams(
            dimension_semantics=("arbitrary",), vmem_limit_bytes=VMEM_LIMIT),
        name="router",
    )(s1, s2)


def _peer_kernel(hnt_ref, u_ref, vt_ref, rank_ref, a2_ref, cnt_ref, a1_ref, x1_ref, nw_ref,
                 o_ref, acc_ref, act0_ref, act1_ref):
    s = pl.program_id(1)
    last = pl.num_programs(1) - 1

    @pl.when(s == 0)
    def _():
        acc_ref[...] = jnp.zeros_like(acc_ref)

    def gated(act_r, js, c):
        lanes = slice(c * LANES, (c + 1) * LANES)
        w = [None] * len(js)
        for h in range(PEER_HEADS):
            rows = slice(h * PEER_N_KEYS, (h + 1) * PEER_N_KEYS)
            rank_h, a2_h = rank_ref[rows, lanes], a2_ref[rows, lanes]
            for k, j in enumerate(js):
                r = j * PEER_HEADS + h
                term = jnp.where(rank_h < cnt_ref[r:r + 1, lanes], a2_h * a1_ref[r:r + 1, lanes], 0.0)
                w[k] = term if w[k] is None else w[k] + term
        return [(jax.nn.gelu(act_r[j * PEER_N_KEYS:(j + 1) * PEER_N_KEYS, lanes]) * w[k]).astype(BF16)
                for k, j in enumerate(js)]

    def stages(act_r, act_w):
        per = MXU_WIDTH // LANES
        n_keys = PEER_TE // PEER_N_KEYS
        for p in range(PEER_TQ // MXU_WIDTH):
            lanes = slice(p * MXU_WIDTH, (p + 1) * MXU_WIDTH)
            if act_r is not None:
                cols = []
                for cc in range(per):
                    blocks = []
                    for j0 in range(0, n_keys, GATE_KEYS):
                        blocks += gated(act_r, range(j0, j0 + GATE_KEYS), p * per + cc)
                    cols.append(jnp.concatenate(blocks, axis=0))
                wact = jnp.concatenate(cols, axis=1)
            if act_w is not None:
                act_w[:, lanes] = jnp.dot(u_ref[:, :D_MODEL], hnt_ref[:, lanes], preferred_element_type=F32)
            if act_r is not None:
                acc_ref[:, lanes] += jnp.dot(vt_ref[:, :PEER_TE], wact, preferred_element_type=F32)

    @pl.when(s == 0)
    def _():
        stages(None, act0_ref)

    @pl.when((s % 2 == 0) & (s > 0) & (s < last))
    def _():
        stages(act1_ref, act0_ref)

    @pl.when(s % 2 == 1)
    def _():
        stages(act0_ref, act1_ref)

    @pl.when(s == last)
    def _():
        stages(act1_ref, None)
        o_ref[...] = _rms(x1_ref[...] + acc_ref[...].T, nw_ref[...])


def _peer(hnt, u_bf, vt_bf, rank, a2, cnt, a1, x1, nw):
    t = x1.shape[0]
    n_exp = u_bf.shape[0]
    nrow = PEER_HEADS * PEER_N_KEYS
    rows_per_step = PEER_HEADS * (PEER_TE // PEER_N_KEYS)
    n_tiles = n_exp // PEER_TE
    assert n_tiles % 2 == 0, "the act buffers alternate with the step parity"
    tile = lambda s, lag: jnp.clip(s - lag, 0, n_tiles - 1)
    tok_tab = pl.BlockSpec((nrow, PEER_TQ), lambda i, s: (0, i))
    exp_tab = pl.BlockSpec((rows_per_step, PEER_TQ), lambda i, s: (tile(s, PEER_SKEW), i))
    tok = pl.BlockSpec((PEER_TQ, D_MODEL), lambda i, s: (i, 0))
    once = lambda spec: pl.BlockSpec(spec.block_shape, spec.index_map, pipeline_mode=pl.Buffered(1))
    return pl.pallas_call(
        _peer_kernel,
        grid=(t // PEER_TQ, n_tiles + PEER_SKEW),
        in_specs=[
            pl.BlockSpec((D_MODEL, PEER_TQ), lambda i, s: (0, i)),
            pl.BlockSpec((PEER_TE, D_MODEL + PITCH_PAD), lambda i, s: (tile(s, 0), 0)),
            pl.BlockSpec((None, D_MODEL, PEER_TE + PITCH_PAD), lambda i, s: (tile(s, PEER_SKEW), 0, 0)),
            once(tok_tab), once(tok_tab), exp_tab, exp_tab, once(tok),
            pl.BlockSpec((1, D_MODEL), lambda i, s: (0, 0)),
        ],
        out_specs=tok,
        out_shape=jax.ShapeDtypeStruct((t, D_MODEL), F32),
        scratch_shapes=[pltpu.VMEM((D_MODEL, PEER_TQ), F32),
                        pltpu.VMEM((PEER_TE, PEER_TQ), F32),
                        pltpu.VMEM((PEER_TE, PEER_TQ), F32)],
        compiler_params=pltpu.CompilerParams(
            dimension_semantics=("arbitrary", "arbitrary"), vmem_limit_bytes=VMEM_LIMIT),
        name="peer",
    )(hnt, u_bf, vt_bf, rank, a2, cnt, a1, x1, nw)


def _block_diag_chunks(w):
    per = GATE_CHUNK // LRU_HEAD_DIM
    w4 = w.reshape(LRU_HEADS // per, per, LRU_HEAD_DIM, LRU_HEAD_DIM)
    bd = jnp.einsum('cgij,gk->cgikj', w4, jnp.eye(per, dtype=w.dtype))
    return bd.reshape(LRU_HEADS // per, GATE_CHUNK, GATE_CHUNK)


def kernel(x, norm_mix_w, w_in, lru_conv_w, lru_conv_b, lru_wa, lru_ba, lru_wx, lru_bx, lru_lambda, ssd_conv_w, ssd_conv_b, ssd_dt_bias, ssd_a_log, ssd_d, ssd_norm_w, w_out, norm_ffn_w, peer_wq, peer_sub_keys, peer_u, peer_v, norm_final_w):
    bsz, seq, _ = x.shape
    assert norm_mix_w.shape[0] == 1, "one layer"
    t = bsz * seq
    assert seq % SEQ_TILE == 0 and t % PEER_TQ == 0 and t % IN_TILE == 0 and t % SCORE_TILE == 0
    x2 = x.reshape(t, D_MODEL)
    row = lambda v: v.reshape(1, -1).astype(F32)

    w_in_pad = jnp.pad(w_in[0], ((0, 0), (0, PROJ_PAD - w_in.shape[-1]))).astype(BF16)
    wg = jnp.concatenate([_block_diag_chunks(lru_wa[0]), _block_diag_chunks(lru_wx[0])], axis=-1).astype(BF16)
    pad_heads = lambda v: jnp.pad(v.reshape(1, -1).astype(F32), ((0, 0), (0, DT_PAD - SSD_HEADS)))
    expand = (jnp.arange(DT_PAD)[:, None] == (jnp.arange(SSD_WIDTH) // SSD_HEAD_DIM)[None, :]).astype(F32)
    dexp = jnp.repeat(ssd_d[0].astype(F32), SSD_HEAD_DIM).reshape(1, -1)
    wq = peer_wq[0].reshape(D_MODEL, PEER_HEADS, 2, PEER_HALF).transpose(0, 2, 1, 3)
    wq = wq.reshape(D_MODEL, 2 * PEER_HEADS * PEER_HALF).astype(BF16)
    keys = peer_sub_keys[0]
    eye = jnp.eye(PEER_HEADS, dtype=keys.dtype)
    k1 = jnp.einsum('hnd,hg->nhgd', keys[:, 0], eye).reshape(PEER_N_KEYS * PEER_HEADS, -1).astype(BF16)
    k2 = jnp.einsum('hnd,hg->hngd', keys[:, 1], eye).reshape(PEER_HEADS * PEER_N_KEYS, -1).astype(BF16)
    pad_lanes = lambda v: jnp.pad(v, [(0, 0)] * (v.ndim - 1) + [(0, PITCH_PAD)])
    u_bf = pad_lanes(peer_u[0]).astype(BF16)
    vt_bf = pad_lanes(peer_v[0].reshape(-1, PEER_TE, D_MODEL).transpose(0, 2, 1)).astype(BF16)

    proj = _in_proj(x2, row(norm_mix_w[0]), w_in_pad)
    ycat = _mixer(proj, bsz, seq,
                  lru_conv_w[0].astype(F32), row(lru_conv_b[0]), wg, row(lru_ba[0]), row(lru_bx[0]),
                  row(lru_lambda[0]), ssd_conv_w[0].astype(F32), row(ssd_conv_b[0]),
                  pad_heads(ssd_dt_bias[0]), pad_heads(ssd_a_log[0]), dexp, row(ssd_norm_w[0]), expand)
    x1, hnt, s1, s2 = _scores(ycat, x2, w_out[0].astype(BF16), row(norm_ffn_w[0]), wq, k1, k2)
    cnt, a1, rank, a2 = _router(s1, s2)
    out = _peer(hnt, u_bf, vt_bf, rank, a2, cnt, a1, x1, row(norm_final_w))
    return out.reshape(bsz, seq, D_MODEL)
```

```python
import functools
import math

import jax
import jax.numpy as jnp
from jax import lax
from jax.experimental import pallas as pl
from jax.experimental.pallas import tpu as pltpu

F32 = jnp.float32
BF16 = jnp.bfloat16

D_MODEL = 2048
CONV_WIDTH = 4
LRU_WIDTH = 1024
LRU_HEADS = 16
LRU_HEAD_DIM = 64
LRU_C = 8.0
SSD_WIDTH = 1024
SSD_HEAD_DIM = 64
SSD_HEADS = 16
SSD_GROUPS = 2
SSD_STATE = 128
SSD_CONV_DIM = SSD_WIDTH + 2 * SSD_GROUPS * SSD_STATE
PEER_HEADS = 8
PEER_N_KEYS = 128
PEER_HALF = 128
PEER_TOPK = 16
EPS = 1e-6

LANES = 128
SUBLANES = 8
MXU_WIDTH = 256
DT_PAD = LANES
PROJ_PAD = 3 * 1024 + SSD_CONV_DIM + DT_PAD
GATE_CHUNK = 256

IN_TILE = 256
SEQ_TILE = 256
SSD_CHUNK = 128
SCORE_TILE = 256
ROUTER_TILE = 128
PEER_TQ = 512
PEER_TE = 1024
GATE_KEYS = 2
PEER_SKEW = 1
VMEM_LIMIT = 56 * 1024 * 1024
PITCH_PAD = LANES


def _rms(x, w):
    return x * lax.rsqrt(jnp.mean(x * x, axis=-1, keepdims=True) + EPS) * w


def _in_proj_kernel(x_ref, nw_ref, w_ref, o_ref):
    hn = _rms(x_ref[...], nw_ref[...])
    o_ref[...] = jnp.dot(hn.astype(BF16), w_ref[...], preferred_element_type=F32)


def _in_proj(x2, norm_w, w_in_pad):
    t = x2.shape[0]
    return pl.pallas_call(
        _in_proj_kernel,
        grid=(t // IN_TILE,),
        in_specs=[
            pl.BlockSpec((IN_TILE, D_MODEL), lambda i: (i, 0)),
            pl.BlockSpec((1, D_MODEL), lambda i: (0, 0)),
            pl.BlockSpec((D_MODEL, PROJ_PAD), lambda i: (0, 0), pipeline_mode=pl.Buffered(1)),
        ],
        out_specs=pl.BlockSpec((IN_TILE, PROJ_PAD), lambda i: (i, 0)),
        out_shape=jax.ShapeDtypeStruct((t, PROJ_PAD), F32),
        compiler_params=pltpu.CompilerParams(
            dimension_semantics=("arbitrary",), vmem_limit_bytes=VMEM_LIMIT),
        name="in_proj",
    )(x2, norm_w, w_in_pad)


def _shift_rows(x, d, fill):
    rolled = pltpu.roll(x, d, axis=0)
    row = lax.broadcasted_iota(jnp.int32, x.shape, 0)
    return jnp.where(row < d, fill, rolled)


def _causal_conv(cur, prev8, w_ref, b_ref):
    row8 = lax.broadcasted_iota(jnp.int32, prev8.shape, 0)
    out = cur * w_ref[CONV_WIDTH - 1:CONV_WIDTH, :] + b_ref[...]
    for d in range(1, CONV_WIDTH):
        rolled = pltpu.roll(cur, d, axis=0)
        top = jnp.where(row8 < d, pltpu.roll(prev8, d, axis=0), rolled[:SUBLANES])
        shifted = jnp.concatenate([top, rolled[SUBLANES:]], axis=0)
        k = CONV_WIDTH - 1 - d
        out = out + shifted * w_ref[k:k + 1, :]
    return out


def _mixer_kernel(lx_ref, lg_ref, z_ref, xbc_ref, dt_ref,
                  lcw_ref, lcb_ref, wg_ref, ba_ref, bx_ref, lam_ref,
                  scw_ref, scb_ref, dtb_ref, alog_ref, dexp_ref, nw_ref, expand_ref,
                  y_ref,
                  lprev_ref, sprev_ref, h_ref, state_ref):
    ts = lx_ref.shape[0]

    @pl.when(pl.program_id(1) == 0)
    def _():
        lprev_ref[...] = jnp.zeros_like(lprev_ref)
        sprev_ref[...] = jnp.zeros_like(sprev_ref)
        h_ref[...] = jnp.zeros_like(h_ref)
        state_ref[...] = jnp.zeros_like(state_ref)

    lx = lx_ref[...]
    xl = _causal_conv(lx, lprev_ref[...], lcw_ref, lcb_ref)
    lprev_ref[...] = lx[ts - SUBLANES:, :]
    xl_bf = xl.astype(BF16)
    ra, ri = [], []
    for c in range(LRU_WIDTH // GATE_CHUNK):
        g = jnp.dot(xl_bf[:, c * GATE_CHUNK:(c + 1) * GATE_CHUNK], wg_ref[c],
                    preferred_element_type=F32)
        ra.append(g[:, :GATE_CHUNK])
        ri.append(g[:, GATE_CHUNK:])
    r = jax.nn.sigmoid(jnp.concatenate(ra, axis=1) + ba_ref[...])
    i = jax.nn.sigmoid(jnp.concatenate(ri, axis=1) + bx_ref[...])
    log_a = -LRU_C * r * jax.nn.softplus(-lam_ref[...])
    a = jnp.exp(log_a)
    b2 = jnp.tanh(-log_a) * (1.0 + a * a)
    b = jnp.where(b2 > 0.0, b2 * lax.rsqrt(b2), 0.0) * (i * xl)
    in_group = lax.broadcasted_iota(jnp.int32, a.shape, 0) % SUBLANES
    d = 1
    while d < SUBLANES:
        b = a * jnp.where(in_group < d, 0.0, pltpu.roll(b, d, axis=0)) + b
        a = a * jnp.where(in_group < d, 1.0, pltpu.roll(a, d, axis=0))
        d *= 2
    h = h_ref[0:1, :]
    groups = []
    for g in range(ts // SUBLANES):
        rows = slice(g * SUBLANES, (g + 1) * SUBLANES)
        hg = a[rows] * h + b[rows]
        groups.append(hg)
        h = hg[SUBLANES - 1:SUBLANES, :]
    hs = jnp.concatenate(groups, axis=0)
    h_ref[...] = jnp.broadcast_to(h, h_ref.shape)
    y_ref[:, :LRU_WIDTH] = (hs * jax.nn.gelu(lg_ref[...])).astype(y_ref.dtype)

    xbc_raw = xbc_ref[...]
    xbc = jax.nn.silu(_causal_conv(xbc_raw, sprev_ref[...], scw_ref, scb_ref))
    sprev_ref[...] = xbc_raw[ts - SUBLANES:, :]
    xs = xbc[:, :SSD_WIDTH]
    dt = jax.nn.softplus(dt_ref[...] + dtb_ref[...])
    adt = -jnp.exp(alog_ref[...]) * dt
    expand = expand_ref[...]

    def widen(v):
        return jnp.dot(v, expand, precision=lax.Precision.HIGHEST, preferred_element_type=F32)

    xc = xs * widen(dt)
    tri = (lax.broadcasted_iota(jnp.int32, (SSD_CHUNK, SSD_CHUNK), 0)
           >= lax.broadcasted_iota(jnp.int32, (SSD_CHUNK, SSD_CHUNK), 1))
    hpg = SSD_HEADS // SSD_GROUPS
    gw = hpg * SSD_HEAD_DIM
    y_chunks = []
    for c in range(ts // SSD_CHUNK):
        rows = slice(c * SSD_CHUNK, (c + 1) * SSD_CHUNK)
        acs = adt[rows]
        d = 1
        while d < SSD_CHUNK:
            acs = acs + _shift_rows(acs, d, 0.0)
            d *= 2
        acs_t = acs.T
        total = acs[SSD_CHUNK - 1:SSD_CHUNK, :]
        e_in = widen(jnp.exp(acs))
        e_out = widen(jnp.exp(total - acs))
        e_all = widen(jnp.exp(total) * jnp.ones((SUBLANES, 1), F32))[0:1, :]
        xc_c = xc[rows]
        xdec = (xc_c * e_out).astype(BF16)
        xc_bf = xc_c.astype(BF16)
        y_parts = []
        for g in range(SSD_GROUPS):
            bg = xbc[rows, SSD_WIDTH + g * SSD_STATE:SSD_WIDTH + (g + 1) * SSD_STATE].astype(BF16)
            cg = xbc[rows, SSD_WIDTH + (SSD_GROUPS + g) * SSD_STATE:
                     SSD_WIDTH + (SSD_GROUPS + g + 1) * SSD_STATE].astype(BF16)
            scores = lax.dot_general(cg, bg, (((1,), (1,)), ((), ())), preferred_element_type=F32)
            gs = slice(g * gw, (g + 1) * gw)
            st = state_ref[:, gs]
            y_off = jnp.dot(cg, st.astype(BF16), preferred_element_type=F32) * e_in[:, gs]
            diag = []
            for hh in range(hpg):
                h = g * hpg + hh
                lm = jnp.where(tri, jnp.exp(acs[:, h:h + 1] - acs_t[h:h + 1, :]), 0.0)
                p = (scores * lm).astype(BF16)
                diag.append(jnp.dot(p, xc_bf[:, h * SSD_HEAD_DIM:(h + 1) * SSD_HEAD_DIM],
                                    preferred_element_type=F32))
            y_parts.append(jnp.concatenate(diag, axis=1) + y_off)
            new = lax.dot_general(bg, xdec[:, gs], (((0,), (0,)), ((), ())),
                                  preferred_element_type=F32)
            state_ref[:, gs] = st * e_all[:, gs] + new
        y_chunks.append(jnp.concatenate(y_parts, axis=1))
    y = jnp.concatenate(y_chunks, axis=0) + dexp_ref[...] * xs
    yg = y * jax.nn.silu(z_ref[...])
    y_ref[:, LRU_WIDTH:] = _rms(yg, nw_ref[...]).astype(y_ref.dtype)


def _mixer(proj, bsz, seq, lcw, lcb, wg, ba, bx, lam, scw, scb, dtb, alog, dexp, nw, expand):
    nj = seq // SEQ_TILE

    def col(width, idx):
        return pl.BlockSpec((SEQ_TILE, width), lambda b, j: (b * nj + j, idx))

    def full(arr):
        nd = arr.ndim
        return pl.BlockSpec(arr.shape, lambda b, j: (0,) * nd)

    consts = (lcw, lcb, wg, ba, bx, lam, scw, scb, dtb, alog, dexp, nw, expand)
    return pl.pallas_call(
        _mixer_kernel,
        grid=(bsz, nj),
        in_specs=[col(1024, 0), col(1024, 1), col(1024, 2), col(SSD_CONV_DIM, 2),
                  col(DT_PAD, (3 * 1024 + SSD_CONV_DIM) // DT_PAD)] + [full(c) for c in consts],
        out_specs=pl.BlockSpec((SEQ_TILE, 2 * 1024), lambda b, j: (b * nj + j, 0)),
        out_shape=jax.ShapeDtypeStruct((bsz * seq, 2 * 1024), BF16),
        scratch_shapes=[
            pltpu.VMEM((SUBLANES, LRU_WIDTH), F32),
            pltpu.VMEM((SUBLANES, SSD_CONV_DIM), F32),
            pltpu.VMEM((SUBLANES, LRU_WIDTH), F32),
            pltpu.VMEM((SSD_STATE, SSD_WIDTH), F32),
        ],
        compiler_params=pltpu.CompilerParams(
            dimension_semantics=("arbitrary", "arbitrary"), vmem_limit_bytes=VMEM_LIMIT),
        name="mixer",
    )(proj, proj, proj, proj, proj, *consts)


def _sort_desc(v):
    v = list(v)
    n = len(v)
    k = 2
    while k <= n:
        j = k // 2
        while j >= 1:
            for i in range(n):
                l = i ^ j
                if l > i:
                    hi, lo = jnp.maximum(v[i], v[l]), jnp.minimum(v[i], v[l])
                    v[i], v[l] = (hi, lo) if (i & k) == 0 else (lo, hi)
            j //= 2
        k *= 2
    return v


def _merge_top(a, b):
    n = len(a)
    v = [jnp.maximum(a[i], b[n - 1 - i]) for i in range(n)]
    j = n // 2
    while j >= 1:
        for i in range(n):
            l = i ^ j
            if l > i:
                v[i], v[l] = jnp.maximum(v[i], v[l]), jnp.minimum(v[i], v[l])
        j //= 2
    return v


def _top_sorted(vals):
    groups = [_sort_desc(vals[i:i + PEER_TOPK]) for i in range(0, len(vals), PEER_TOPK)]
    while len(groups) > 1:
        nxt = [_merge_top(groups[i], groups[i + 1]) for i in range(0, len(groups) - 1, 2)]
        if len(groups) % 2:
            nxt.append(groups[-1])
        groups = nxt
    return groups[0]


def _scores_kernel(y_ref, x_ref, wout_ref, nw_ref, wq_ref, k1_ref, k2_ref,
                   x1_ref, hnt_ref, s1_ref, s2_ref):
    x1 = x_ref[...] + jnp.dot(y_ref[...], wout_ref[...], preferred_element_type=F32)
    x1_ref[...] = x1
    hn = _rms(x1, nw_ref[...]).astype(BF16)
    hnt_ref[...] = hn.T
    q = jnp.dot(hn, wq_ref[...], preferred_element_type=F32).astype(BF16)
    half = PEER_HEADS * PEER_HALF
    nt = (((1,), (1,)), ((), ()))
    s1_ref[...] = lax.dot_general(k1_ref[...], q[:, :half], nt, preferred_element_type=F32)
    for c in range(half // MXU_WIDTH):
        blk = slice(c * MXU_WIDTH, (c + 1) * MXU_WIDTH)
        s2_ref[blk, :] = lax.dot_general(k2_ref[blk, blk], q[:, half + c * MXU_WIDTH:half + (c + 1) * MXU_WIDTH],
                                         nt, preferred_element_type=F32)


def _scores(ycat, x2, w_out, nw, wq, k1, k2):
    t = x2.shape[0]
    nrow = PEER_HEADS * PEER_N_KEYS
    tok = pl.BlockSpec((SCORE_TILE, D_MODEL), lambda i: (i, 0))
    tab = pl.BlockSpec((nrow, SCORE_TILE), lambda i: (0, i))

    def full(arr):
        nd = arr.ndim
        return pl.BlockSpec(arr.shape, lambda i: (0,) * nd, pipeline_mode=pl.Buffered(1))

    return pl.pallas_call(
        _scores_kernel,
        grid=(t // SCORE_TILE,),
        in_specs=[tok, tok, full(w_out), full(nw), full(wq), full(k1), full(k2)],
        out_specs=[tok, pl.BlockSpec((D_MODEL, SCORE_TILE), lambda i: (0, i)), tab, tab],
        out_shape=[
            jax.ShapeDtypeStruct((t, D_MODEL), F32),
            jax.ShapeDtypeStruct((D_MODEL, t), BF16),
            jax.ShapeDtypeStruct((nrow, t), F32),
            jax.ShapeDtypeStruct((nrow, t), F32),
        ],
        compiler_params=pltpu.CompilerParams(
            dimension_semantics=("arbitrary",), vmem_limit_bytes=VMEM_LIMIT),
        name="scores",
    )(ycat, x2, w_out, nw, wq, k1, k2)


def _router_kernel(s1_ref, s2_ref, cnt_ref, a1_ref, rank_ref, a2_ref):
    tq = s1_ref.shape[1]
    s1 = s1_ref[...].reshape(PEER_N_KEYS, PEER_HEADS, tq)
    s2 = s2_ref[...]

    a = _top_sorted([s1[n] for n in range(PEER_N_KEYS)])
    b = _top_sorted([s2_ref[pl.ds(n, PEER_HEADS, stride=PEER_N_KEYS), :] for n in range(PEER_N_KEYS)])
    cands = [a[i] + b[j] for i in range(PEER_TOPK) for j in range(PEER_TOPK)
             if (i + 1) * (j + 1) <= PEER_TOPK]
    neg = jnp.full_like(a[0], -jnp.inf)
    tau = _top_sorted(cands + [neg] * (-len(cands) % PEER_TOPK))[PEER_TOPK - 1]
    m = a[0] + b[0]
    z = jnp.zeros_like(m)
    for c in cands:
        z = z + jnp.where(c >= tau, jnp.exp(c - m), 0.0)
    inv_z = 1.0 / z

    cnt = jnp.zeros_like(s1)
    for j in range(PEER_TOPK):
        cnt = cnt + jnp.where(s1 + b[j] >= tau, 1.0, 0.0)
    cnt_ref[...] = cnt.reshape(PEER_N_KEYS * PEER_HEADS, tq)
    a1_ref[...] = jnp.exp(s1 - a[0]).reshape(PEER_N_KEYS * PEER_HEADS, tq)
    for h in range(PEER_HEADS):
        rows = slice(h * PEER_N_KEYS, (h + 1) * PEER_N_KEYS)
        s2h = s2[rows, :]
        rank = jnp.zeros_like(s2h)
        for k in range(PEER_TOPK):
            rank = rank + jnp.where(b[k][h:h + 1, :] > s2h, 1.0, 0.0)
        rank_ref[rows, :] = rank.astype(rank_ref.dtype)
        a2_ref[rows, :] = (jnp.exp(s2h - b[0][h:h + 1, :]) * inv_z[h:h + 1, :]).astype(a2_ref.dtype)


def _router(s1, s2):
    nrow, t = s1.shape
    tab = pl.BlockSpec((nrow, ROUTER_TILE), lambda i: (0, i))
    return pl.pallas_call(
        _router_kernel,
        grid=(t // ROUTER_TILE,),
        in_specs=[tab, tab],
        out_specs=[tab, tab, tab, tab],
        out_shape=[jax.ShapeDtypeStruct((nrow, t), F32)] * 4,
        compiler_params=pltpu.CompilerParams(
            dimension_semantics=("arbitrary",), vmem_limit_bytes=VMEM_LIMIT),
        name="router",
    )(s1, s2)


def _peer_kernel(hnt_ref, u_ref, vt_ref, rank_ref, a2_ref, cnt_ref, a1_ref, x1_ref, nw_ref,
                 o_ref, acc_ref, act0_ref, act1_ref):
    s = pl.program_id(1)
    last = pl.num_programs(1) - 1

    @pl.when(s == 0)
    def _():
        acc_ref[...] = jnp.zeros_like(acc_ref)

    def gated(act_r, js, c):
        lanes = slice(c * LANES, (c + 1) * LANES)
        w = [None] * len(js)
        for h in range(PEER_HEADS):
            rows = slice(h * PEER_N_KEYS, (h + 1) * PEER_N_KEYS)
            rank_h, a2_h = rank_ref[rows, lanes], a2_ref[rows, lanes]
            for k, j in enumerate(js):
                r = j * PEER_HEADS + h
                term = jnp.where(rank_h < cnt_ref[r:r + 1, lanes], a2_h * a1_ref[r:r + 1, lanes], 0.0)
                w[k] = term if w[k] is None else w[k] + term
        return [(jax.nn.gelu(act_r[j * PEER_N_KEYS:(j + 1) * PEER_N_KEYS, lanes]) * w[k]).astype(BF16)
                for k, j in enumerate(js)]

    def stages(act_r, act_w):
        per = MXU_WIDTH // LANES
        n_keys = PEER_TE // PEER_N_KEYS
        for p in range(PEER_TQ // MXU_WIDTH):
            lanes = slice(p * MXU_WIDTH, (p + 1) * MXU_WIDTH)
            if act_r is not None:
                cols = []
                for cc in range(per):
                    blocks = []
                    for j0 in range(0, n_keys, GATE_KEYS):
                        blocks += gated(act_r, range(j0, j0 + GATE_KEYS), p * per + cc)
                    cols.append(jnp.concatenate(blocks, axis=0))
                wact = jnp.concatenate(cols, axis=1)
            if act_w is not None:
                act_w[:, lanes] = jnp.dot(u_ref[:, :D_MODEL], hnt_ref[:, lanes], preferred_element_type=F32)
            if act_r is not None:
                acc_ref[:, lanes] += jnp.dot(vt_ref[:, :PEER_TE], wact, preferred_element_type=F32)

    @pl.when(s == 0)
    def _():
        stages(None, act0_ref)

    @pl.when((s % 2 == 0) & (s > 0) & (s < last))
    def _():
        stages(act1_ref, act0_ref)

    @pl.when(s % 2 == 1)
    def _():
        stages(act0_ref, act1_ref)

    @pl.when(s == last)
    def _():
        stages(act1_ref, None)
        o_ref[...] = _rms(x1_ref[...] + acc_ref[...].T, nw_ref[...])


def _peer(hnt, u_bf, vt_bf, rank, a2, cnt, a1, x1, nw):
    t = x1.shape[0]
    n_exp = u_bf.shape[0]
    nrow = PEER_HEADS * PEER_N_KEYS
    rows_per_step = PEER_HEADS * (PEER_TE // PEER_N_KEYS)
    n_tiles = n_exp // PEER_TE
    assert n_tiles % 2 == 0, "the act buffers alternate with the step parity"
    tile = lambda s, lag: jnp.clip(s - lag, 0, n_tiles - 1)
    tok_tab = pl.BlockSpec((nrow, PEER_TQ), lambda i, s: (0, i))
    exp_tab = pl.BlockSpec((rows_per_step, PEER_TQ), lambda i, s: (tile(s, PEER_SKEW), i))
    tok = pl.BlockSpec((PEER_TQ, D_MODEL), lambda i, s: (i, 0))
    once = lambda spec: pl.BlockSpec(spec.block_shape, spec.index_map, pipeline_mode=pl.Buffered(1))
    return pl.pallas_call(
        _peer_kernel,
        grid=(t // PEER_TQ, n_tiles + PEER_SKEW),
        in_specs=[
            pl.BlockSpec((D_MODEL, PEER_TQ), lambda i, s: (0, i)),
            pl.BlockSpec((PEER_TE, D_MODEL + PITCH_PAD), lambda i, s: (tile(s, 0), 0)),
            pl.BlockSpec((None, D_MODEL, PEER_TE + PITCH_PAD), lambda i, s: (tile(s, PEER_SKEW), 0, 0)),
            once(tok_tab), once(tok_tab), exp_tab, exp_tab, once(tok),
            pl.BlockSpec((1, D_MODEL), lambda i, s: (0, 0)),
        ],
        out_specs=tok,
        out_shape=jax.ShapeDtypeStruct((t, D_MODEL), F32),
        scratch_shapes=[pltpu.VMEM((D_MODEL, PEER_TQ), F32),
                        pltpu.VMEM((PEER_TE, PEER_TQ), F32),
                        pltpu.VMEM((PEER_TE, PEER_TQ), F32)],
        compiler_params=pltpu.CompilerParams(
            dimension_semantics=("arbitrary", "arbitrary"), vmem_limit_bytes=VMEM_LIMIT),
        name="peer",
    )(hnt, u_bf, vt_bf, rank, a2, cnt, a1, x1, nw)


def _block_diag_chunks(w):
    per = GATE_CHUNK // LRU_HEAD_DIM
    w4 = w.reshape(LRU_HEADS // per, per, LRU_HEAD_DIM, LRU_HEAD_DIM)
    bd = jnp.einsum('cgij,gk->cgikj', w4, jnp.eye(per, dtype=w.dtype))
    return bd.reshape(LRU_HEADS // per, GATE_CHUNK, GATE_CHUNK)


def kernel(x, norm_mix_w, w_in, lru_conv_w, lru_conv_b, lru_wa, lru_ba, lru_wx, lru_bx, lru_lambda, ssd_conv_w, ssd_conv_b, ssd_dt_bias, ssd_a_log, ssd_d, ssd_norm_w, w_out, norm_ffn_w, peer_wq, peer_sub_keys, peer_u, peer_v, norm_final_w):
    bsz, seq, _ = x.shape
    assert norm_mix_w.shape[0] == 1, "one layer"
    t = bsz * seq
    assert seq % SEQ_TILE == 0 and t % PEER_TQ == 0 and t % IN_TILE == 0 and t % SCORE_TILE == 0
    x2 = x.reshape(t, D_MODEL)
    row = lambda v: v.reshape(1, -1).astype(F32)

    w_in_pad = jnp.pad(w_in[0], ((0, 0), (0, PROJ_PAD - w_in.shape[-1]))).astype(BF16)
    wg = jnp.concatenate([_block_diag_chunks(lru_wa[0]), _block_diag_chunks(lru_wx[0])], axis=-1).astype(BF16)
    pad_heads = lambda v: jnp.pad(v.reshape(1, -1).astype(F32), ((0, 0), (0, DT_PAD - SSD_HEADS)))
    expand = (jnp.arange(DT_PAD)[:, None] == (jnp.arange(SSD_WIDTH) // SSD_HEAD_DIM)[None, :]).astype(F32)
    dexp = jnp.repeat(ssd_d[0].astype(F32), SSD_HEAD_DIM).reshape(1, -1)
    wq = peer_wq[0].reshape(D_MODEL, PEER_HEADS, 2, PEER_HALF).transpose(0, 2, 1, 3)
    wq = wq.reshape(D_MODEL, 2 * PEER_HEADS * PEER_HALF).astype(BF16)
    keys = peer_sub_keys[0]
    eye = jnp.eye(PEER_HEADS, dtype=keys.dtype)
    k1 = jnp.einsum('hnd,hg->nhgd', keys[:, 0], eye).reshape(PEER_N_KEYS * PEER_HEADS, -1).astype(BF16)
    k2 = jnp.einsum('hnd,hg->hngd', keys[:, 1], eye).reshape(PEER_HEADS * PEER_N_KEYS, -1).astype(BF16)
    pad_lanes = lambda v: jnp.concatenate([v, v[..., :PITCH_PAD]], axis=-1)
    u_bf = pad_lanes(peer_u[0]).astype(BF16)
    vt_bf = pad_lanes(peer_v[0].reshape(-1, PEER_TE, D_MODEL).transpose(0, 2, 1)).astype(BF16)

    proj = _in_proj(x2, row(norm_mix_w[0]), w_in_pad)
    ycat = _mixer(proj, bsz, seq,
                  lru_conv_w[0].astype(F32), row(lru_conv_b[0]), wg, row(lru_ba[0]), row(lru_bx[0]),
                  row(lru_lambda[0]), ssd_conv_w[0].astype(F32), row(ssd_conv_b[0]),
                  pad_heads(ssd_dt_bias[0]), pad_heads(ssd_a_log[0]), dexp, row(ssd_norm_w[0]), expand)
    x1, hnt, s1, s2 = _scores(ycat, x2, w_out[0].astype(BF16), row(norm_ffn_w[0]), wq, k1, k2)
    cnt, a1, rank, a2 = _router(s1, s2)
    out = _peer(hnt, u_bf, vt_bf, rank, a2, cnt, a1, x1, row(norm_final_w))
    return out.reshape(bsz, seq, D_MODEL)
```

```python
import jax
import jax.numpy as jnp
from jax import lax
from jax.experimental import pallas as pl
from jax.experimental.pallas import tpu as pltpu

F32 = jnp.float32
BF16 = jnp.bfloat16

D_MODEL = 2048
CONV_WIDTH = 4
LRU_WIDTH = 1024
LRU_HEADS = 16
LRU_HEAD_DIM = 64
LRU_C = 8.0
SSD_WIDTH = 1024
SSD_HEAD_DIM = 64
SSD_HEADS = 16
SSD_GROUPS = 2
SSD_STATE = 128
SSD_CONV_DIM = SSD_WIDTH + 2 * SSD_GROUPS * SSD_STATE
PEER_HEADS = 8
PEER_N_KEYS = 128
PEER_HALF = 128
PEER_TOPK = 16
EPS = 1e-6

LANES = 128
SUBLANES = 8
MXU_WIDTH = 256
DT_PAD = LANES
PROJ_PAD = 3 * 1024 + SSD_CONV_DIM + DT_PAD
GATE_CHUNK = 256

IN_TILE = 512
SEQ_TILE = 256
SSD_CHUNK = 128
SCORE_TILE = 256
ROUTER_TILE = 128
PEER_TQ = 512
PEER_TE = 1024
GATE_KEYS = 2
PEER_SKEW = 1
VMEM_LIMIT = 56 * 1024 * 1024
PITCH_PAD = LANES


def _rms(x, w):
    return x * lax.rsqrt(jnp.mean(x * x, axis=-1, keepdims=True) + EPS) * w


def _in_proj_kernel(x_ref, nw_ref, w_ref, o_ref):
    hn = _rms(x_ref[...], nw_ref[...])
    o_ref[...] = jnp.dot(hn.astype(BF16), w_ref[...], preferred_element_type=F32)


def _in_proj(x2, norm_w, w_in_pad):
    t = x2.shape[0]
    return pl.pallas_call(
        _in_proj_kernel,
        grid=(t // IN_TILE,),
        in_specs=[
            pl.BlockSpec((IN_TILE, D_MODEL), lambda i: (i, 0)),
            pl.BlockSpec((1, D_MODEL), lambda i: (0, 0)),
            pl.BlockSpec((D_MODEL, PROJ_PAD), lambda i: (0, 0), pipeline_mode=pl.Buffered(1)),
        ],
        out_specs=pl.BlockSpec((IN_TILE, PROJ_PAD), lambda i: (i, 0)),
        out_shape=jax.ShapeDtypeStruct((t, PROJ_PAD), F32),
        compiler_params=pltpu.CompilerParams(
            dimension_semantics=("arbitrary",), vmem_limit_bytes=VMEM_LIMIT),
        name="in_proj",
    )(x2, norm_w, w_in_pad)


def _shift_rows(x, d, fill):
    rolled = pltpu.roll(x, d, axis=0)
    row = lax.broadcasted_iota(jnp.int32, x.shape, 0)
    return jnp.where(row < d, fill, rolled)


def _causal_conv(cur, prev8, w_ref, b_ref):
    row8 = lax.broadcasted_iota(jnp.int32, prev8.shape, 0)
    out = cur * w_ref[CONV_WIDTH - 1:CONV_WIDTH, :] + b_ref[...]
    for d in range(1, CONV_WIDTH):
        rolled = pltpu.roll(cur, d, axis=0)
        top = jnp.where(row8 < d, pltpu.roll(prev8, d, axis=0), rolled[:SUBLANES])
        shifted = jnp.concatenate([top, rolled[SUBLANES:]], axis=0)
        k = CONV_WIDTH - 1 - d
        out = out + shifted * w_ref[k:k + 1, :]
    return out


def _mixer_kernel(lx_ref, lg_ref, z_ref, xbc_ref, dt_ref,
                  lcw_ref, lcb_ref, wg_ref, ba_ref, bx_ref, lam_ref,
                  scw_ref, scb_ref, dtb_ref, alog_ref, dexp_ref, nw_ref, expand_ref,
                  y_ref,
                  lprev_ref, sprev_ref, h_ref, state_ref):
    ts = lx_ref.shape[0]

    @pl.when(pl.program_id(1) == 0)
    def _():
        lprev_ref[...] = jnp.zeros_like(lprev_ref)
        sprev_ref[...] = jnp.zeros_like(sprev_ref)
        h_ref[...] = jnp.zeros_like(h_ref)
        state_ref[...] = jnp.zeros_like(state_ref)

    lx = lx_ref[...]
    xl = _causal_conv(lx, lprev_ref[...], lcw_ref, lcb_ref)
    lprev_ref[...] = lx[ts - SUBLANES:, :]
    xl_bf = xl.astype(BF16)
    ra, ri = [], []
    for c in range(LRU_WIDTH // GATE_CHUNK):
        g = jnp.dot(xl_bf[:, c * GATE_CHUNK:(c + 1) * GATE_CHUNK], wg_ref[c],
                    preferred_element_type=F32)
        ra.append(g[:, :GATE_CHUNK])
        ri.append(g[:, GATE_CHUNK:])
    r = jax.nn.sigmoid(jnp.concatenate(ra, axis=1) + ba_ref[...])
    i = jax.nn.sigmoid(jnp.concatenate(ri, axis=1) + bx_ref[...])
    log_a = -LRU_C * r * jax.nn.softplus(-lam_ref[...])
    a = jnp.exp(log_a)
    b2 = jnp.tanh(-log_a) * (1.0 + a * a)
    b = jnp.where(b2 > 0.0, b2 * lax.rsqrt(b2), 0.0) * (i * xl)
    in_group = lax.broadcasted_iota(jnp.int32, a.shape, 0) % SUBLANES
    d = 1
    while d < SUBLANES:
        b = a * jnp.where(in_group < d, 0.0, pltpu.roll(b, d, axis=0)) + b
        a = a * jnp.where(in_group < d, 1.0, pltpu.roll(a, d, axis=0))
        d *= 2
    h = h_ref[0:1, :]
    groups = []
    for g in range(ts // SUBLANES):
        rows = slice(g * SUBLANES, (g + 1) * SUBLANES)
        hg = a[rows] * h + b[rows]
        groups.append(hg)
        h = hg[SUBLANES - 1:SUBLANES, :]
    hs = jnp.concatenate(groups, axis=0)
    h_ref[...] = jnp.broadcast_to(h, h_ref.shape)
    y_ref[:, :LRU_WIDTH] = (hs * jax.nn.gelu(lg_ref[...])).astype(y_ref.dtype)

    xbc_raw = xbc_ref[...]
    xbc = jax.nn.silu(_causal_conv(xbc_raw, sprev_ref[...], scw_ref, scb_ref))
    sprev_ref[...] = xbc_raw[ts - SUBLANES:, :]
    xs = xbc[:, :SSD_WIDTH]
    dt = jax.nn.softplus(dt_ref[...] + dtb_ref[...])
    adt = -jnp.exp(alog_ref[...]) * dt
    expand = expand_ref[...]

    def widen(v):
        return jnp.dot(v, expand, precision=lax.Precision.HIGHEST, preferred_element_type=F32)

    xc = xs * widen(dt)
    tri = (lax.broadcasted_iota(jnp.int32, (SSD_CHUNK, SSD_CHUNK), 0)
           >= lax.broadcasted_iota(jnp.int32, (SSD_CHUNK, SSD_CHUNK), 1))
    hpg = SSD_HEADS // SSD_GROUPS
    gw = hpg * SSD_HEAD_DIM
    y_chunks = []
    for c in range(ts // SSD_CHUNK):
        rows = slice(c * SSD_CHUNK, (c + 1) * SSD_CHUNK)
        acs = adt[rows]
        d = 1
        while d < SSD_CHUNK:
            acs = acs + _shift_rows(acs, d, 0.0)
            d *= 2
        acs_t = acs.T
        total = acs[SSD_CHUNK - 1:SSD_CHUNK, :]
        e_in = widen(jnp.exp(acs))
        e_out = widen(jnp.exp(total - acs))
        e_all = widen(jnp.exp(total) * jnp.ones((SUBLANES, 1), F32))[0:1, :]
        xc_c = xc[rows]
        xdec = (xc_c * e_out).astype(BF16)
        xc_bf = xc_c.astype(BF16)
        y_parts = []
        for g in range(SSD_GROUPS):
            bg = xbc[rows, SSD_WIDTH + g * SSD_STATE:SSD_WIDTH + (g + 1) * SSD_STATE].astype(BF16)
            cg = xbc[rows, SSD_WIDTH + (SSD_GROUPS + g) * SSD_STATE:
                     SSD_WIDTH + (SSD_GROUPS + g + 1) * SSD_STATE].astype(BF16)
            scores = lax.dot_general(cg, bg, (((1,), (1,)), ((), ())), preferred_element_type=F32)
            gs = slice(g * gw, (g + 1) * gw)
            st = state_ref[:, gs]
            y_off = jnp.dot(cg, st.astype(BF16), preferred_element_type=F32) * e_in[:, gs]
            diag = []
            for hh in range(hpg):
                h = g * hpg + hh
                lm = jnp.where(tri, jnp.exp(acs[:, h:h + 1] - acs_t[h:h + 1, :]), 0.0)
                p = (scores * lm).astype(BF16)
                diag.append(jnp.dot(p, xc_bf[:, h * SSD_HEAD_DIM:(h + 1) * SSD_HEAD_DIM],
                                    preferred_element_type=F32))
            y_parts.append(jnp.concatenate(diag, axis=1) + y_off)
            new = lax.dot_general(bg, xdec[:, gs], (((0,), (0,)), ((), ())),
                                  preferred_element_type=F32)
            state_ref[:, gs] = st * e_all[:, gs] + new
        y_chunks.append(jnp.concatenate(y_parts, axis=1))
    y = jnp.concatenate(y_chunks, axis=0) + dexp_ref[...] * xs
    yg = y * jax.nn.silu(z_ref[...])
    y_ref[:, LRU_WIDTH:] = _rms(yg, nw_ref[...]).astype(y_ref.dtype)


def _mixer(proj, bsz, seq, lcw, lcb, wg, ba, bx, lam, scw, scb, dtb, alog, dexp, nw, expand):
    nj = seq // SEQ_TILE

    def col(width, idx):
        return pl.BlockSpec((SEQ_TILE, width), lambda b, j: (b * nj + j, idx))

    def full(arr):
        nd = arr.ndim
        return pl.BlockSpec(arr.shape, lambda b, j: (0,) * nd)

    consts = (lcw, lcb, wg, ba, bx, lam, scw, scb, dtb, alog, dexp, nw, expand)
    return pl.pallas_call(
        _mixer_kernel,
        grid=(bsz, nj),
        in_specs=[col(1024, 0), col(1024, 1), col(1024, 2), col(SSD_CONV_DIM, 2),
                  col(DT_PAD, (3 * 1024 + SSD_CONV_DIM) // DT_PAD)] + [full(c) for c in consts],
        out_specs=pl.BlockSpec((SEQ_TILE, 2 * 1024), lambda b, j: (b * nj + j, 0)),
        out_shape=jax.ShapeDtypeStruct((bsz * seq, 2 * 1024), BF16),
        scratch_shapes=[
            pltpu.VMEM((SUBLANES, LRU_WIDTH), F32),
            pltpu.VMEM((SUBLANES, SSD_CONV_DIM), F32),
            pltpu.VMEM((SUBLANES, LRU_WIDTH), F32),
            pltpu.VMEM((SSD_STATE, SSD_WIDTH), F32),
        ],
        compiler_params=pltpu.CompilerParams(
            dimension_semantics=("arbitrary", "arbitrary"), vmem_limit_bytes=VMEM_LIMIT),
        name="mixer",
    )(proj, proj, proj, proj, proj, *consts)


def _sort_desc(v):
    v = list(v)
    n = len(v)
    k = 2
    while k <= n:
        j = k // 2
        while j >= 1:
            for i in range(n):
                l = i ^ j
                if l > i:
                    hi, lo = jnp.maximum(v[i], v[l]), jnp.minimum(v[i], v[l])
                    v[i], v[l] = (hi, lo) if (i & k) == 0 else (lo, hi)
            j //= 2
        k *= 2
    return v


def _merge_top(a, b):
    n = len(a)
    v = [jnp.maximum(a[i], b[n - 1 - i]) for i in range(n)]
    j = n // 2
    while j >= 1:
        for i in range(n):
            l = i ^ j
            if l > i:
                v[i], v[l] = jnp.maximum(v[i], v[l]), jnp.minimum(v[i], v[l])
        j //= 2
    return v


def _top_sorted(vals):
    groups = [_sort_desc(vals[i:i + PEER_TOPK]) for i in range(0, len(vals), PEER_TOPK)]
    while len(groups) > 1:
        nxt = [_merge_top(groups[i], groups[i + 1]) for i in range(0, len(groups) - 1, 2)]
        if len(groups) % 2:
            nxt.append(groups[-1])
        groups = nxt
    return groups[0]


def _scores_kernel(y_ref, x_ref, wout_ref, nw_ref, wq_ref, k1_ref, k2_ref,
                   x1_ref, hnt_ref, s1_ref, s2_ref):
    x1 = x_ref[...] + jnp.dot(y_ref[...], wout_ref[...], preferred_element_type=F32)
    x1_ref[...] = x1
    hn = _rms(x1, nw_ref[...]).astype(BF16)
    hnt_ref[...] = hn.T
    q = jnp.dot(hn, wq_ref[...], preferred_element_type=F32).astype(BF16)
    half = PEER_HEADS * PEER_HALF
    nt = (((1,), (1,)), ((), ()))
    s1_ref[...] = lax.dot_general(k1_ref[...], q[:, :half], nt, preferred_element_type=F32)
    for c in range(half // MXU_WIDTH):
        blk = slice(c * MXU_WIDTH, (c + 1) * MXU_WIDTH)
        s2_ref[blk, :] = lax.dot_general(k2_ref[blk, blk], q[:, half + c * MXU_WIDTH:half + (c + 1) * MXU_WIDTH],
                                         nt, preferred_element_type=F32)


def _scores(ycat, x2, w_out, nw, wq, k1, k2):
    t = x2.shape[0]
    nrow = PEER_HEADS * PEER_N_KEYS
    tok = pl.BlockSpec((SCORE_TILE, D_MODEL), lambda i: (i, 0))
    tab = pl.BlockSpec((nrow, SCORE_TILE), lambda i: (0, i))

    def full(arr):
        nd = arr.ndim
        return pl.BlockSpec(arr.shape, lambda i: (0,) * nd, pipeline_mode=pl.Buffered(1))

    return pl.pallas_call(
        _scores_kernel,
        grid=(t // SCORE_TILE,),
        in_specs=[tok, tok, full(w_out), full(nw), full(wq), full(k1), full(k2)],
        out_specs=[tok, pl.BlockSpec((D_MODEL, SCORE_TILE), lambda i: (0, i)), tab, tab],
        out_shape=[
            jax.ShapeDtypeStruct((t, D_MODEL), F32),
            jax.ShapeDtypeStruct((D_MODEL, t), BF16),
            jax.ShapeDtypeStruct((nrow, t), F32),
            jax.ShapeDtypeStruct((nrow, t), F32),
        ],
        compiler_params=pltpu.CompilerParams(
            dimension_semantics=("arbitrary",), vmem_limit_bytes=VMEM_LIMIT),
        name="scores",
    )(ycat, x2, w_out, nw, wq, k1, k2)


def _router_kernel(s1_ref, s2_ref, cnt_ref, a1_ref, rank_ref, a2_ref):
    tq = s1_ref.shape[1]
    s1 = s1_ref[...].reshape(PEER_N_KEYS, PEER_HEADS, tq)
    s2 = s2_ref[...]

    a = _top_sorted([s1[n] for n in range(PEER_N_KEYS)])
    b = _top_sorted([s2_ref[pl.ds(n, PEER_HEADS, stride=PEER_N_KEYS), :] for n in range(PEER_N_KEYS)])
    cands = [a[i] + b[j] for i in range(PEER_TOPK) for j in range(PEER_TOPK)
             if (i + 1) * (j + 1) <= PEER_TOPK]
    neg = jnp.full_like(a[0], -jnp.inf)
    tau = _top_sorted(cands + [neg] * (-len(cands) % PEER_TOPK))[PEER_TOPK - 1]
    m = a[0] + b[0]
    z = jnp.zeros_like(m)
    for c in cands:
        z = z + jnp.where(c >= tau, jnp.exp(c - m), 0.0)
    inv_z = 1.0 / z

    cnt = jnp.zeros_like(s1)
    for j in range(PEER_TOPK):
        cnt = cnt + jnp.where(s1 + b[j] >= tau, 1.0, 0.0)
    cnt_ref[...] = cnt.reshape(PEER_N_KEYS * PEER_HEADS, tq)
    a1_ref[...] = jnp.exp(s1 - a[0]).reshape(PEER_N_KEYS * PEER_HEADS, tq)
    for h in range(PEER_HEADS):
        rows = slice(h * PEER_N_KEYS, (h + 1) * PEER_N_KEYS)
        s2h = s2[rows, :]
        rank = jnp.zeros_like(s2h)
        for k in range(PEER_TOPK):
            rank = rank + jnp.where(b[k][h:h + 1, :] > s2h, 1.0, 0.0)
        rank_ref[rows, :] = rank.astype(rank_ref.dtype)
        a2_ref[rows, :] = (jnp.exp(s2h - b[0][h:h + 1, :]) * inv_z[h:h + 1, :]).astype(a2_ref.dtype)


def _router(s1, s2):
    nrow, t = s1.shape
    tab = pl.BlockSpec((nrow, ROUTER_TILE), lambda i: (0, i))
    return pl.pallas_call(
        _router_kernel,
        grid=(t // ROUTER_TILE,),
        in_specs=[tab, tab],
        out_specs=[tab, tab, tab, tab],
        out_shape=[jax.ShapeDtypeStruct((nrow, t), F32)] * 4,
        compiler_params=pltpu.CompilerParams(
            dimension_semantics=("arbitrary",), vmem_limit_bytes=VMEM_LIMIT),
        name="router",
    )(s1, s2)


def _peer_kernel(hnt_ref, u_ref, vt_ref, rank_ref, a2_ref, cnt_ref, a1_ref, x1_ref, nw_ref,
                 o_ref, acc_ref, act0_ref, act1_ref):
    s = pl.program_id(1)
    last = pl.num_programs(1) - 1

    @pl.when(s == 0)
    def _():
        acc_ref[...] = jnp.zeros_like(acc_ref)

    def gated(act_r, js, c):
        lanes = slice(c * LANES, (c + 1) * LANES)
        w = [None] * len(js)
        for h in range(PEER_HEADS):
            rows = slice(h * PEER_N_KEYS, (h + 1) * PEER_N_KEYS)
            rank_h, a2_h = rank_ref[rows, lanes], a2_ref[rows, lanes]
            for k, j in enumerate(js):
                r = j * PEER_HEADS + h
                term = jnp.where(rank_h < cnt_ref[r:r + 1, lanes], a2_h * a1_ref[r:r + 1, lanes], 0.0)
                w[k] = term if w[k] is None else w[k] + term
        return [(jax.nn.gelu(act_r[j * PEER_N_KEYS:(j + 1) * PEER_N_KEYS, lanes]) * w[k]).astype(BF16)
                for k, j in enumerate(js)]

    def stages(act_r, act_w):
        per = MXU_WIDTH // LANES
        n_keys = PEER_TE // PEER_N_KEYS
        for p in range(PEER_TQ // MXU_WIDTH):
            lanes = slice(p * MXU_WIDTH, (p + 1) * MXU_WIDTH)
            if act_r is not None:
                cols = []
                for cc in range(per):
                    blocks = []
                    for j0 in range(0, n_keys, GATE_KEYS):
                        blocks += gated(act_r, range(j0, j0 + GATE_KEYS), p * per + cc)
                    cols.append(jnp.concatenate(blocks, axis=0))
                wact = jnp.concatenate(cols, axis=1)
            if act_w is not None:
                act_w[:, lanes] = jnp.dot(u_ref[:, :D_MODEL], hnt_ref[:, lanes], preferred_element_type=F32)
            if act_r is not None:
                acc_ref[:, lanes] += jnp.dot(vt_ref[:, :PEER_TE], wact, preferred_element_type=F32)

    @pl.when(s == 0)
    def _():
        stages(None, act0_ref)

    @pl.when((s % 2 == 0) & (s > 0) & (s < last))
    def _():
        stages(act1_ref, act0_ref)

    @pl.when(s % 2 == 1)
    def _():
        stages(act0_ref, act1_ref)

    @pl.when(s == last)
    def _():
        stages(act1_ref, None)
        o_ref[...] = _rms(x1_ref[...] + acc_ref[...].T, nw_ref[...])


def _peer(hnt, u_bf, vt_bf, rank, a2, cnt, a1, x1, nw):
    t = x1.shape[0]
    n_exp = u_bf.shape[0]
    nrow = PEER_HEADS * PEER_N_KEYS
    rows_per_step = PEER_HEADS * (PEER_TE // PEER_N_KEYS)
    n_tiles = n_exp // PEER_TE
    assert n_tiles % 2 == 0, "the act buffers alternate with the step parity"
    tile = lambda s, lag: jnp.clip(s - lag, 0, n_tiles - 1)
    tok_tab = pl.BlockSpec((nrow, PEER_TQ), lambda i, s: (0, i))
    exp_tab = pl.BlockSpec((rows_per_step, PEER_TQ), lambda i, s: (tile(s, PEER_SKEW), i))
    tok = pl.BlockSpec((PEER_TQ, D_MODEL), lambda i, s: (i, 0))
    once = lambda spec: pl.BlockSpec(spec.block_shape, spec.index_map, pipeline_mode=pl.Buffered(1))
    return pl.pallas_call(
        _peer_kernel,
        grid=(t // PEER_TQ, n_tiles + PEER_SKEW),
        in_specs=[
            pl.BlockSpec((D_MODEL, PEER_TQ), lambda i, s: (0, i)),
            pl.BlockSpec((PEER_TE, D_MODEL + PITCH_PAD), lambda i, s: (tile(s, 0), 0)),
            pl.BlockSpec((None, D_MODEL, PEER_TE + PITCH_PAD), lambda i, s: (tile(s, PEER_SKEW), 0, 0)),
            once(tok_tab), once(tok_tab), exp_tab, exp_tab, once(tok),
            pl.BlockSpec((1, D_MODEL), lambda i, s: (0, 0)),
        ],
        out_specs=tok,
        out_shape=jax.ShapeDtypeStruct((t, D_MODEL), F32),
        scratch_shapes=[pltpu.VMEM((D_MODEL, PEER_TQ), F32),
                        pltpu.VMEM((PEER_TE, PEER_TQ), F32),
                        pltpu.VMEM((PEER_TE, PEER_TQ), F32)],
        compiler_params=pltpu.CompilerParams(
            dimension_semantics=("arbitrary", "arbitrary"), vmem_limit_bytes=VMEM_LIMIT),
        name="peer",
    )(hnt, u_bf, vt_bf, rank, a2, cnt, a1, x1, nw)


def _block_diag_chunks(w):
    per = GATE_CHUNK // LRU_HEAD_DIM
    w4 = w.reshape(LRU_HEADS // per, per, LRU_HEAD_DIM, LRU_HEAD_DIM)
    bd = jnp.einsum('cgij,gk->cgikj', w4, jnp.eye(per, dtype=w.dtype))
    return bd.reshape(LRU_HEADS // per, GATE_CHUNK, GATE_CHUNK)


def kernel(x, norm_mix_w, w_in, lru_conv_w, lru_conv_b, lru_wa, lru_ba, lru_wx, lru_bx, lru_lambda, ssd_conv_w, ssd_conv_b, ssd_dt_bias, ssd_a_log, ssd_d, ssd_norm_w, w_out, norm_ffn_w, peer_wq, peer_sub_keys, peer_u, peer_v, norm_final_w):
    bsz, seq, _ = x.shape
    assert norm_mix_w.shape[0] == 1, "one layer"
    t = bsz * seq
    assert seq % SEQ_TILE == 0 and t % PEER_TQ == 0 and t % IN_TILE == 0 and t % SCORE_TILE == 0
    x2 = x.reshape(t, D_MODEL)
    row = lambda v: v.reshape(1, -1).astype(F32)

    w_in_pad = jnp.pad(w_in[0], ((0, 0), (0, PROJ_PAD - w_in.shape[-1]))).astype(BF16)
    wg = jnp.concatenate([_block_diag_chunks(lru_wa[0]), _block_diag_chunks(lru_wx[0])], axis=-1).astype(BF16)
    pad_heads = lambda v: jnp.pad(v.reshape(1, -1).astype(F32), ((0, 0), (0, DT_PAD - SSD_HEADS)))
    expand = (jnp.arange(DT_PAD)[:, None] == (jnp.arange(SSD_WIDTH) // SSD_HEAD_DIM)[None, :]).astype(F32)
    dexp = jnp.repeat(ssd_d[0].astype(F32), SSD_HEAD_DIM).reshape(1, -1)
    wq = peer_wq[0].reshape(D_MODEL, PEER_HEADS, 2, PEER_HALF).transpose(0, 2, 1, 3)
    wq = wq.reshape(D_MODEL, 2 * PEER_HEADS * PEER_HALF).astype(BF16)
    keys = peer_sub_keys[0]
    eye = jnp.eye(PEER_HEADS, dtype=keys.dtype)
    k1 = jnp.einsum('hnd,hg->nhgd', keys[:, 0], eye).reshape(PEER_N_KEYS * PEER_HEADS, -1).astype(BF16)
    k2 = jnp.einsum('hnd,hg->hngd', keys[:, 1], eye).reshape(PEER_HEADS * PEER_N_KEYS, -1).astype(BF16)
    pad_lanes = lambda v: jnp.concatenate([v, v[..., :PITCH_PAD]], axis=-1)
    u_bf = pad_lanes(peer_u[0]).astype(BF16)
    vt_bf = pad_lanes(peer_v[0].reshape(-1, PEER_TE, D_MODEL).transpose(0, 2, 1)).astype(BF16)

    proj = _in_proj(x2, row(norm_mix_w[0]), w_in_pad)
    ycat = _mixer(proj, bsz, seq,
                  lru_conv_w[0].astype(F32), row(lru_conv_b[0]), wg, row(lru_ba[0]), row(lru_bx[0]),
                  row(lru_lambda[0]), ssd_conv_w[0].astype(F32), row(ssd_conv_b[0]),
                  pad_heads(ssd_dt_bias[0]), pad_heads(ssd_a_log[0]), dexp, row(ssd_norm_w[0]), expand)
    x1, hnt, s1, s2 = _scores(ycat, x2, w_out[0].astype(BF16), row(norm_ffn_w[0]), wq, k1, k2)
    cnt, a1, rank, a2 = _router(s1, s2)
    out = _peer(hnt, u_bf, vt_bf, rank, a2, cnt, a1, x1, row(norm_final_w))
    return out.reshape(bsz, seq, D_MODEL)
```
